```python
import math
import jax, jax.numpy as jnp
from jax import lax
import numpy as np

D_MODEL = 2048
BATCH = 2
SEQ = 4096
DEPTH = 1

MIX_WIDTH = D_MODEL
HGRN_WIDTH = MIX_WIDTH // 2
HGRN_HEAD_DIM = 128
HGRN_HEADS = HGRN_WIDTH // HGRN_HEAD_DIM
MLSTM_WIDTH = MIX_WIDTH - HGRN_WIDTH
MLSTM_HEADS = 4
MLSTM_HEAD_DIM = MLSTM_WIDTH // MLSTM_HEADS
CHUNK = 64
CONV_WIDTH = 5
D_FF = 256 * ((8 * D_MODEL // 3 + 255) // 256)
DN_ALPHA = (2.0 * DEPTH) ** 0.25
DN_BETA = (8.0 * DEPTH) ** -0.25
LN_EPS = 1e-5
NORM_EPS = 1e-6
M_INIT = -1e30
IN_SPLITS = (HGRN_WIDTH, HGRN_WIDTH, HGRN_WIDTH, HGRN_WIDTH, HGRN_WIDTH,
             MLSTM_WIDTH, MLSTM_WIDTH, MLSTM_WIDTH, MLSTM_WIDTH,
             MLSTM_HEADS, MLSTM_HEADS, MLSTM_HEADS, MLSTM_HEADS)
IN_COLS = 5 * HGRN_WIDTH + 4 * MLSTM_WIDTH + 4 * MLSTM_HEADS

kernel_name = 'bidir_hgrn2_mlstm_macaron_deepnorm'


def _layer_norm(x, g, b):
    xf = x.astype(jnp.float32)
    mu = jnp.mean(xf, axis=-1, keepdims=True)
    var = jnp.mean(jnp.square(xf - mu), axis=-1, keepdims=True)
    y = (xf - mu) * lax.rsqrt(var + LN_EPS)
    return (y * g.astype(jnp.float32) + b.astype(jnp.float32)).astype(x.dtype)


def _swiglu(x, w1, w3, w2):
    return (jax.nn.silu(x @ w1) * (x @ w3)) @ w2


def _to_heads(t, n):
    b_, t_, w = t.shape
    return t.reshape(b_, t_, n, w // n).transpose(0, 2, 1, 3)


def _merge_heads(t):
    b_, h_, t_, d = t.shape
    return t.transpose(0, 2, 1, 3).reshape(b_, t_, h_ * d)


def _chunk(t):
    b_, h_, t_ = t.shape[:3]
    return jnp.moveaxis(t.reshape(b_, h_, t_ // CHUNK, CHUNK, *t.shape[3:]), 2, 0)


def _unchunk(t):
    t = jnp.moveaxis(t, 0, 2)
    return t.reshape(t.shape[0], t.shape[1], t.shape[2] * t.shape[3], *t.shape[4:])


def _flip(t):
    return jnp.flip(t, axis=2)


def _hgrn2_scan(q, k, v, logf):
    b_, h_, _, dk = q.shape
    dv = v.shape[-1]
    mask = jnp.tril(jnp.ones((CHUNK, CHUNK), dtype=bool))

    def step(state, inp):
        qc, kc, vc, gc = inp
        bcum = jnp.cumsum(gc, axis=-2)
        o_inter = jnp.einsum('bhtd,bhde->bhte', qc * jnp.exp(bcum), state)
        rel = bcum[..., :, None, :] - bcum[..., None, :, :]
        decay = jnp.exp(jnp.where(mask[:, :, None], rel, -jnp.inf))
        attn = jnp.einsum('bhtd,bhsd,bhtsd->bhts', qc, kc, decay)
        o = o_inter + jnp.einsum('bhts,bhse->bhte', attn, vc)
        b_last = bcum[..., -1:, :]
        state = (jnp.exp(b_last[..., 0, :])[..., None] * state
                 + jnp.einsum('bhsd,bhse->bhde', kc * jnp.exp(b_last - bcum), vc))
        return state, o

    s0 = jnp.zeros((b_, h_, dk, dv), jnp.float32)
    _, o = lax.scan(step, s0, (_chunk(q), _chunk(k), _chunk(v), _chunk(logf)))
    return _unchunk(o)


def _mlstm_scan(q, k, v, ig, lf):
    b_, h_, _, dk = q.shape
    dv = v.shape[-1]
    mask = jnp.tril(jnp.ones((CHUNK, CHUNK), dtype=bool))

    def step(carry, inp):
        c_st, n_st, m_st = carry
        qc, kc, vc, igc, lfc = inp
        bcum = jnp.cumsum(lfc, axis=-1)
        dmat = jnp.where(mask, bcum[..., :, None] - bcum[..., None, :] + igc[..., None, :], -jnp.inf)
        m_inter = bcum + m_st[..., None]
        m_t = jnp.maximum(m_inter, jnp.max(dmat, axis=-1))
        inter_scale = jnp.exp(m_inter - m_t)
        sc = jnp.einsum('bhtd,bhsd->bhts', qc, kc) * jnp.exp(dmat - m_t[..., None])
        num = jnp.einsum('bhts,bhse->bhte', sc, vc) + inter_scale[..., None] * jnp.einsum('bhtd,bhde->bhte', qc, c_st)
        den = jnp.sum(sc, axis=-1) + inter_scale * jnp.einsum('bhtd,bhd->bht', qc, n_st)
        h = num / jnp.maximum(jnp.abs(den), jnp.exp(-m_t))[..., None]
        b_last = bcum[..., -1]
        w = b_last[..., None] - bcum + igc
        m_new = jnp.maximum(b_last + m_st, jnp.max(w, axis=-1))
        carry_scale = jnp.exp(b_last + m_st - m_new)
        kw = kc * jnp.exp(w - m_new[..., None])[..., None]
        c_new = carry_scale[..., None, None] * c_st + jnp.einsum('bhsd,bhse->bhde', kw, vc)
        n_new = carry_scale[..., None] * n_st + jnp.sum(kw, axis=-2)
        return (c_new, n_new, m_new), h

    carry0 = (jnp.zeros((b_, h_, dk, dv), jnp.float32),
              jnp.zeros((b_, h_, dk), jnp.float32),
              jnp.full((b_, h_), M_INIT, jnp.float32))
    _, h = lax.scan(step, carry0, (_chunk(q), _chunk(k), _chunk(v), _chunk(ig), _chunk(lf)))
    return _unchunk(h)


def _centred_dwconv(t, w, b):
    y = lax.conv_general_dilated(t, w[:, None, :], window_strides=(1,),
                                 padding=[(CONV_WIDTH // 2, CONV_WIDTH // 2)],
                                 dimension_numbers=('NWC', 'WIO', 'NWC'),
                                 feature_group_count=t.shape[-1])
    return y + b


def _mixer(x, layer, w_in, hgrn_lb, hgrn_norm_g, conv_w, conv_b, ig_b, fg_b, mlstm_norm_g, w_out):
    f32 = jnp.float32
    z = x @ w_in
    offsets = [int(o) for o in np.cumsum(IN_SPLITS)[:-1]]
    (hq, hi, hg, hf_fw, hf_bw, mq, mk, mv, mo, mi_fw, mi_bw, mf_fw, mf_bw) = jnp.split(z, offsets, axis=-1)

    lb = jnp.cumsum(jax.nn.softmax(hgrn_lb.astype(f32), axis=1), axis=1)[:, layer]
    q_h = _to_heads(jax.nn.silu(hq.astype(f32)) * (HGRN_HEAD_DIM ** -0.5), HGRN_HEADS)
    v_h = _to_heads(hi.astype(f32), HGRN_HEADS)

    def _forget(zf, lbd):
        f = lbd + (1.0 - lbd) * jax.nn.sigmoid(zf.astype(f32))
        return _to_heads(jnp.log(f), HGRN_HEADS), _to_heads(1.0 - f, HGRN_HEADS)

    logf_fw, k_fw = _forget(hf_fw, lb[0])
    logf_bw, k_bw = _forget(hf_bw, lb[1])
    o_h = (_hgrn2_scan(q_h, k_fw, v_h, logf_fw)
           + _flip(_hgrn2_scan(_flip(q_h), _flip(k_bw), _flip(v_h), _flip(logf_bw))))
    o_h = o_h * lax.rsqrt(jnp.mean(jnp.square(o_h), axis=-1, keepdims=True) + NORM_EPS)
    y_h = _merge_heads(o_h) * hgrn_norm_g.astype(f32) * jax.nn.silu(hg.astype(f32))

    qk = jax.nn.silu(_centred_dwconv(jnp.concatenate([mq, mk], axis=-1), conv_w, conv_b))
    mq_c, mk_c = jnp.split(qk, 2, axis=-1)
    q_m = _to_heads(mq_c.astype(f32), MLSTM_HEADS) * (MLSTM_HEAD_DIM ** -0.5)
    k_m = _to_heads(mk_c.astype(f32), MLSTM_HEADS)
    v_m = _to_heads(mv.astype(f32), MLSTM_HEADS)
    ig_fw = (mi_fw.astype(f32) + ig_b[0].astype(f32)).transpose(0, 2, 1)
    ig_bw = (mi_bw.astype(f32) + ig_b[1].astype(f32)).transpose(0, 2, 1)
    lf_fw = jax.nn.log_sigmoid(mf_fw.astype(f32) + fg_b[0].astype(f32)).transpose(0, 2, 1)
    lf_bw = jax.nn.log_sigmoid(mf_bw.astype(f32) + fg_b[1].astype(f32)).transpose(0, 2, 1)
    h_m = (_mlstm_scan(q_m, k_m, v_m, ig_fw, lf_fw)
           + _flip(_mlstm_scan(_flip(q_m), _flip(k_m), _flip(v_m), _flip(ig_bw), _flip(lf_bw))))
    mu = jnp.mean(h_m, axis=-1, keepdims=True)
    var = jnp.mean(jnp.square(h_m - mu), axis=-1, keepdims=True)
    h_m = (h_m - mu) * lax.rsqrt(var + NORM_EPS)
    y_m = _merge_heads(h_m) * mlstm_norm_g.astype(f32) * jax.nn.sigmoid(mo.astype(f32))

    y = jnp.concatenate([y_h, y_m], axis=-1).astype(x.dtype)
    return y @ w_out


def setup_inputs(seed: int = 0) -> dict:
    key = jax.random.key(seed)
    ks = jax.random.split(key, 24)
    f32 = jnp.float32
    d_sc = D_MODEL ** -0.5
    ff_sc = D_FF ** -0.5

    def nrm(k, shape, scale):
        return jax.random.normal(k, shape, f32) * scale

    col_scale = jnp.concatenate([
        jnp.ones((HGRN_WIDTH,), f32),
        jnp.full((HGRN_WIDTH,), DN_BETA, f32),
        jnp.ones((3 * HGRN_WIDTH,), f32),
        jnp.ones((2 * MLSTM_WIDTH,), f32),
        jnp.full((MLSTM_WIDTH,), DN_BETA, f32),
        jnp.ones((MLSTM_WIDTH,), f32),
        jnp.full((4 * MLSTM_HEADS,), 0.1, f32),
    ])
    fg_bias = jnp.broadcast_to(jnp.linspace(3.0, 6.0, MLSTM_HEADS, dtype=f32), (DEPTH, 2, MLSTM_HEADS))
    return {
        'x': jax.random.normal(ks[0], (BATCH, SEQ, D_MODEL), f32),
        'ffn1_w1': nrm(ks[1], (DEPTH, D_MODEL, D_FF), d_sc),
        'ffn1_w3': nrm(ks[2], (DEPTH, D_MODEL, D_FF), d_sc),
        'ffn1_w2': nrm(ks[3], (DEPTH, D_FF, D_MODEL), ff_sc * DN_BETA),
        'ln1_g': 1.0 + nrm(ks[4], (DEPTH, D_MODEL), 0.02),
        'ln1_b': nrm(ks[5], (DEPTH, D_MODEL), 0.02),
        'w_in': nrm(ks[6], (DEPTH, D_MODEL, IN_COLS), d_sc) * col_scale,
        'hgrn_lb': nrm(ks[7], (2, DEPTH + 1, HGRN_WIDTH), 0.1),
        'hgrn_norm_g': 1.0 + nrm(ks[8], (DEPTH, HGRN_WIDTH), 0.02),
        'mlstm_conv_w': nrm(ks[9], (DEPTH, CONV_WIDTH, 2 * MLSTM_WIDTH), CONV_WIDTH ** -0.5),
        'mlstm_conv_b': nrm(ks[10], (DEPTH, 2 * MLSTM_WIDTH), 0.02),
        'mlstm_ig_b': nrm(ks[11], (DEPTH, 2, MLSTM_HEADS), 0.1),
        'mlstm_fg_b': fg_bias + nrm(ks[12], (DEPTH, 2, MLSTM_HEADS), 0.1),
        'mlstm_norm_g': 1.0 + nrm(ks[13], (DEPTH, MLSTM_WIDTH), 0.02),
        'w_out': nrm(ks[14], (DEPTH, MIX_WIDTH, D_MODEL), (MIX_WIDTH ** -0.5) * DN_BETA),
        'ln2_g': 1.0 + nrm(ks[15], (DEPTH, D_MODEL), 0.02),
        'ln2_b': nrm(ks[16], (DEPTH, D_MODEL), 0.02),
        'ffn2_w1': nrm(ks[17], (DEPTH, D_MODEL, D_FF), d_sc),
        'ffn2_w3': nrm(ks[18], (DEPTH, D_MODEL, D_FF), d_sc),
        'ffn2_w2': nrm(ks[19], (DEPTH, D_FF, D_MODEL), ff_sc * DN_BETA),
        'ln3_g': 1.0 + nrm(ks[20], (DEPTH, D_MODEL), 0.02),
        'ln3_b': nrm(ks[21], (DEPTH, D_MODEL), 0.02),
    }


def reference(x, ffn1_w1, ffn1_w3, ffn1_w2, ln1_g, ln1_b, w_in, hgrn_lb, hgrn_norm_g,
              mlstm_conv_w, mlstm_conv_b, mlstm_ig_b, mlstm_fg_b, mlstm_norm_g, w_out,
              ln2_g, ln2_b, ffn2_w1, ffn2_w3, ffn2_w2, ln3_g, ln3_b):
    for l in range(DEPTH):
        x = _layer_norm(x * DN_ALPHA + 0.5 * _swiglu(x, ffn1_w1[l], ffn1_w3[l], ffn1_w2[l]), ln1_g[l], ln1_b[l])
        y = _mixer(x, l, w_in[l], hgrn_lb, hgrn_norm_g[l], mlstm_conv_w[l], mlstm_conv_b[l],
                   mlstm_ig_b[l], mlstm_fg_b[l], mlstm_norm_g[l], w_out[l])
        x = _layer_norm(x * DN_ALPHA + y, ln2_g[l], ln2_b[l])
        x = _layer_norm(x * DN_ALPHA + 0.5 * _swiglu(x, ffn2_w1[l], ffn2_w3[l], ffn2_w2[l]), ln3_g[l], ln3_b[l])
    return x
```

```python
import functools

import jax
import jax.numpy as jnp
from jax import lax
from jax.experimental import pallas as pl
from jax.experimental.pallas import tpu as pltpu

F32 = jnp.float32
BF16 = jnp.bfloat16

V7X_VMEM_BYTES = 64 * 1024 * 1024
V7X_LANES = 128
V7X_SUBLANES = 8

HGRN_HEAD_DIM = 128
MLSTM_HEADS = 4
CONV_WIDTH = 5
LN_EPS = 1e-5
NORM_EPS = 1e-6
M_INIT = -1e30
SCAN_CHUNK = 128
DIAG_BLOCK = V7X_SUBLANES


def _vmem_limit(block_bytes, scratch_bytes):
    need = 2 * block_bytes + scratch_bytes
    return int(min(V7X_VMEM_BYTES - (4 << 20), need + (16 << 20)))


def _nt(a, b):
    return lax.dot_general(a, b, (((1,), (1,)), ((), ())), preferred_element_type=F32)


def _tn(a, b):
    return lax.dot_general(a, b, (((0,), (0,)), ((), ())), preferred_element_type=F32)


def _sigmoid(x):
    return 1.0 / (1.0 + jnp.exp(-x))


def _log_sigmoid(x):
    return -(jnp.maximum(-x, 0.0) + jnp.log(1.0 + jnp.exp(-jnp.abs(x))))


def _ffn_ln_kernel(x_ref, w1_ref, w3_ref, w2_ref, g_ref, b_ref, *rest, alpha, emit_bf16):
    if emit_bf16:
        o_ref, obf_ref, xbf_ref, acc_ref = rest
    else:
        o_ref, xbf_ref, acc_ref = rest
    j = pl.program_id(1)

    @pl.when(j == 0)
    def _():
        xbf_ref[...] = x_ref[...].astype(BF16)
        acc_ref[...] = jnp.zeros_like(acc_ref)

    xb = xbf_ref[...]
    h1 = jnp.dot(xb, w1_ref[...], preferred_element_type=F32)
    h3 = jnp.dot(xb, w3_ref[...], preferred_element_type=F32)
    gate = (h1 * _sigmoid(h1) * h3).astype(BF16)
    acc_ref[...] += jnp.dot(gate, w2_ref[...], preferred_element_type=F32)

    @pl.when(j == pl.num_programs(1) - 1)
    def _():
        r = x_ref[...] * alpha + 0.5 * acc_ref[...]
        mu = jnp.mean(r, axis=-1, keepdims=True)
        d = r - mu
        var = jnp.mean(d * d, axis=-1, keepdims=True)
        y = d * lax.rsqrt(var + LN_EPS) * g_ref[...] + b_ref[...]
        o_ref[...] = y
        if emit_bf16:
            obf_ref[...] = y.astype(BF16)


def _ffn_ln(x, w1, w3, w2, g, b, *, alpha, emit_bf16, tm=512, tf=512):
    n, d = x.shape
    f = w1.shape[1]
    assert n % tm == 0 and f % tf == 0
    out_shape = [jax.ShapeDtypeStruct((n, d), F32)]
    out_specs = [pl.BlockSpec((tm, d), lambda i, j: (i, 0))]
    if emit_bf16:
        out_shape.append(jax.ShapeDtypeStruct((n, d), BF16))
        out_specs.append(pl.BlockSpec((tm, d), lambda i, j: (i, 0)))
    block_bytes = tm * d * 4 + 2 * d * tf * 2 + tf * d * 2 + tm * d * 4 + (tm * d * 2 if emit_bf16 else 0)
    scratch_bytes = tm * d * 2 + tm * d * 4
    return pl.pallas_call(
        functools.partial(_ffn_ln_kernel, alpha=alpha, emit_bf16=emit_bf16),
        grid=(n // tm, f // tf),
        in_specs=[
            pl.BlockSpec((tm, d), lambda i, j: (i, 0)),
            pl.BlockSpec((d, tf), lambda i, j: (0, j)),
            pl.BlockSpec((d, tf), lambda i, j: (0, j)),
            pl.BlockSpec((tf, d), lambda i, j: (j, 0)),
            pl.BlockSpec((1, d), lambda i, j: (0, 0)),
            pl.BlockSpec((1, d), lambda i, j: (0, 0)),
        ],
        out_specs=out_specs,
        out_shape=out_shape,
        scratch_shapes=[pltpu.VMEM((tm, d), BF16), pltpu.VMEM((tm, d), F32)],
        compiler_params=pltpu.CompilerParams(
            dimension_semantics=("parallel", "arbitrary"),
            vmem_limit_bytes=_vmem_limit(block_bytes, scratch_bytes)),
        name="ffn_ln",
    )(x, w1, w3, w2, g, b)


def _in_proj_kernel(x_ref, w_ref, wg_ref, z_ref, gates_ref):
    z_ref[...] = jnp.dot(x_ref[...], w_ref[...], preferred_element_type=F32)

    @pl.when(pl.program_id(1) == 0)
    def _():
        gates_ref[...] = jnp.dot(x_ref[...], wg_ref[...], preferred_element_type=F32)


def _in_proj(xbf, w_main, w_gates, *, tm=1024, tn=1024):
    n, d = xbf.shape
    cols = w_main.shape[1]
    gl = w_gates.shape[1]
    assert n % tm == 0 and cols % tn == 0
    block_bytes = tm * d * 2 + d * tn * 2 + d * gl * 2 + tm * tn * 4 + tm * gl * 4
    return pl.pallas_call(
        _in_proj_kernel,
        grid=(n // tm, cols // tn),
        in_specs=[
            pl.BlockSpec((tm, d), lambda i, j: (i, 0)),
            pl.BlockSpec((d, tn), lambda i, j: (0, j)),
            pl.BlockSpec((d, gl), lambda i, j: (0, 0)),
        ],
        out_specs=[
            pl.BlockSpec((tm, tn), lambda i, j: (i, j)),
            pl.BlockSpec((tm, gl), lambda i, j: (i, 0)),
        ],
        out_shape=[jax.ShapeDtypeStruct((n, cols), F32), jax.ShapeDtypeStruct((n, gl), F32)],
        compiler_params=pltpu.CompilerParams(
            dimension_semantics=("parallel", "arbitrary"),
            vmem_limit_bytes=_vmem_limit(block_bytes, 0)),
        name="in_proj",
    )(xbf, w_main, w_gates)


def _cumsum01(tri_bf, g):
    g1 = g.astype(BF16)
    r1 = g - g1.astype(F32)
    g2 = r1.astype(BF16)
    g3 = (r1 - g2.astype(F32)).astype(BF16)
    return (jnp.dot(tri_bf, g1, preferred_element_type=F32)
            + jnp.dot(tri_bf, g2, preferred_element_type=F32)
            + jnp.dot(tri_bf, g3, preferred_element_type=F32))


def _hgrn_chunk(q, k, v, g, st, *, reverse):
    c, d = q.shape
    row = lax.broadcasted_iota(jnp.int32, (c, c), 0)
    col = lax.broadcasted_iota(jnp.int32, (c, c), 1)
    tri = (col >= row) if reverse else (col <= row)
    b = _cumsum01(jnp.where(tri, 1.0, 0.0).astype(BF16), g)
    bend = b[0:1, :] if reverse else b[c - 1:c, :]

    qd = (q * jnp.exp(b)).astype(BF16)
    kd = (k * jnp.exp(bend - b)).astype(BF16)
    vb = v.astype(BF16)
    o = _nt(qd, st.astype(BF16))
    st_new = st * jnp.exp(bend) + _tn(vb, kd)

    t_idx = lax.broadcasted_iota(jnp.int32, (c, d), 0)
    attn = jnp.zeros((c, c), F32)
    m = DIAG_BLOCK
    while m < c:
        pieces = []
        for p in range(c // (2 * m)):
            r = p * 2 * m + (m if reverse else m - 1)
            pieces.append(jnp.broadcast_to(b[r:r + 1, :], (2 * m, d)))
        ref = pieces[0] if len(pieces) == 1 else jnp.concatenate(pieces, axis=0)
        x = jnp.exp(-jnp.abs(b - ref))
        in_right = (t_idx & m) != 0
        q_side = jnp.logical_not(in_right) if reverse else in_right
        ql = jnp.where(q_side, q * x, 0.0).astype(BF16)
        kl = jnp.where(q_side, 0.0, k * x).astype(BF16)
        a_m = _nt(ql, kl)
        shift = (2 * m).bit_length() - 1
        same_pair = (row >> shift) == (col >> shift)
        attn = attn + jnp.where(same_pair, a_m, 0.0)
        m *= 2
    o = o + jnp.dot(attn.astype(BF16), vb, preferred_element_type=F32)

    nb = c // DIAG_BLOCK
    q3 = q.reshape(nb, DIAG_BLOCK, d)
    k3 = k.reshape(nb, DIAG_BLOCK, d)
    v3 = v.reshape(nb, DIAG_BLOCK, d)
    b3 = b.reshape(nb, DIAG_BLOCK, d)
    r_idx = lax.broadcasted_iota(jnp.int32, (nb, DIAG_BLOCK, d), 1)
    od = jnp.zeros((nb, DIAG_BLOCK, d), F32)
    for s in range(DIAG_BLOCK):
        valid = (r_idx <= s) if reverse else (r_idx >= s)
        x = jnp.exp(jnp.where(valid, b3 - b3[:, s:s + 1, :], -jnp.inf))
        p = q3 * k3[:, s:s + 1, :] * x
        od = od + jnp.sum(p, axis=-1, keepdims=True) * v3[:, s:s + 1, :]
    o = o + od.reshape(c, d)
    return o, st_new


def _hgrn_kernel(zq_ref, zi_ref, zg_ref, zff_ref, zfb_ref, lbp_ref, ng_ref, y_ref, ofw_ref, *, chunk):
    t, d = zq_ref.shape
    nc = t // chunk
    lbp = lbp_ref[...]

    def lower_bound(a0, a1):
        mx = jnp.maximum(a0, a1)
        e0 = jnp.exp(a0 - mx)
        e1 = jnp.exp(a1 - mx)
        return e0 / (e0 + e1)

    lb_fw = lower_bound(lbp[0:1, :], lbp[1:2, :])
    lb_bw = lower_bound(lbp[2:3, :], lbp[3:4, :])
    scale = HGRN_HEAD_DIM ** -0.5

    def load(cidx, zf_ref, lb):
        rows = pl.ds(pl.multiple_of(cidx * chunk, chunk), chunk)
        zq = zq_ref[rows, :]
        q = zq * _sigmoid(zq) * scale
        f = lb + (1.0 - lb) * _sigmoid(zf_ref[rows, :])
        return rows, q, 1.0 - f, zi_ref[rows, :], jnp.log(f)

    def fw_body(i, st):
        rows, q, k, v, g = load(i, zff_ref, lb_fw)
        o, st = _hgrn_chunk(q, k, v, g, st, reverse=False)
        ofw_ref[rows, :] = o
        return st

    lax.fori_loop(0, nc, fw_body, jnp.zeros((d, d), F32))

    ng = ng_ref[...]

    def bw_body(i, st):
        rows, q, k, v, g = load(nc - 1 - i, zfb_ref, lb_bw)
        o, st = _hgrn_chunk(q, k, v, g, st, reverse=True)
        o = o + ofw_ref[rows, :]
        o = o * lax.rsqrt(jnp.mean(o * o, axis=-1, keepdims=True) + NORM_EPS)
        zg = zg_ref[rows, :]
        y_ref[rows, :] = (o * ng * (zg * _sigmoid(zg))).astype(y_ref.dtype)
        return st

    lax.fori_loop(0, nc, bw_body, jnp.zeros((d, d), F32))


def _hgrn(z, lbp, norm_g, *, batch, seq, heads, chunk=SCAN_CHUNK):
    d = HGRN_HEAD_DIM
    n = z.shape[0]
    assert n == batch * seq and seq % chunk == 0

    def zspec(off):
        return pl.BlockSpec((seq, d), lambda b, h: (b, off * heads + h))

    block_bytes = 5 * seq * d * 4 + seq * d * 2
    scratch_bytes = seq * d * 4
    return pl.pallas_call(
        functools.partial(_hgrn_kernel, chunk=chunk),
        grid=(batch, heads),
        in_specs=[zspec(0), zspec(1), zspec(2), zspec(3), zspec(4),
                  pl.BlockSpec((4, d), lambda b, h: (0, h)),
                  pl.BlockSpec((1, d), lambda b, h: (0, h))],
        out_specs=pl.BlockSpec((seq, d), lambda b, h: (b, h)),
        out_shape=jax.ShapeDtypeStruct((n, heads * d), BF16),
        scratch_shapes=[pltpu.VMEM((seq, d), F32)],
        compiler_params=pltpu.CompilerParams(
            dimension_semantics=("parallel", "parallel"),
            vmem_limit_bytes=_vmem_limit(block_bytes, scratch_bytes)),
        name="hgrn",
    )(z, z, z, z, z, lbp, norm_g)


def _qk_conv_kernel(x_ref, w_ref, b_ref, o_ref, xp_ref, *, tile, q_blocks, q_scale):
    t, c = x_ref.shape
    pad = V7X_SUBLANES
    half = CONV_WIDTH // 2
    xp_ref[0:pad, :] = jnp.zeros((pad, c), F32)
    xp_ref[pad + t:2 * pad + t, :] = jnp.zeros((pad, c), F32)
    xp_ref[pad:pad + t, :] = x_ref[...]
    w = w_ref[...]
    bias = b_ref[...]
    scale = jnp.where(pl.program_id(1) < q_blocks, q_scale, 1.0)
    win = tile + 2 * pad

    def body(i, carry):
        r0 = pl.multiple_of(i * tile, tile)
        window = xp_ref[pl.ds(r0, win), :]
        acc = bias + w[half:half + 1, :] * window[pad:pad + tile, :]
        for j in range(CONV_WIDTH):
            off = j - half
            if off == 0:
                continue
            shifted = pltpu.roll(window, (win - off) % win, 0)
            acc = acc + w[j:j + 1, :] * shifted[pad:pad + tile, :]
        o_ref[pl.ds(r0, tile), :] = (acc * _sigmoid(acc) * scale).astype(o_ref.dtype)
        return carry

    lax.fori_loop(0, t // tile, body, 0)


def _qk_conv(z, conv_w, conv_b, *, batch, seq, col_block0, width, q_width, q_scale, cblk=256, tile=256):
    n = z.shape[0]
    assert seq % tile == 0 and width % cblk == 0 and q_width % cblk == 0
    block_bytes = seq * cblk * 4 + CONV_WIDTH * cblk * 4 + cblk * 4 + seq * cblk * 2
    scratch_bytes = (seq + 2 * V7X_SUBLANES) * cblk * 4
    return pl.pallas_call(
        functools.partial(_qk_conv_kernel, tile=tile, q_blocks=q_width // cblk, q_scale=q_scale),
        grid=(batch, width // cblk),
        in_specs=[pl.BlockSpec((seq, cblk), lambda b, j: (b, col_block0 + j)),
                  pl.BlockSpec((CONV_WIDTH, cblk), lambda b, j: (0, j)),
                  pl.BlockSpec((1, cblk), lambda b, j: (0, j))],
        out_specs=pl.BlockSpec((seq, cblk), lambda b, j: (b, j)),
        out_shape=jax.ShapeDtypeStruct((n, width), BF16),
        scratch_shapes=[pltpu.VMEM((seq + 2 * V7X_SUBLANES, cblk), F32)],
        compiler_params=pltpu.CompilerParams(
            dimension_semantics=("parallel", "parallel"),
            vmem_limit_bytes=_vmem_limit(block_bytes, scratch_bytes)),
        name="qk_conv",
    )(z, conv_w, conv_b)


def _mlstm_chunk(q, k, v, lf_c, ig_c, lf_r, ig_r, c_ref, n, m, *, reverse):
    c = q.shape[0]
    row = lax.broadcasted_iota(jnp.int32, (c, c), 0)
    col = lax.broadcasted_iota(jnp.int32, (c, c), 1)
    tri = (col >= row) if reverse else (col <= row)
    tri_t = (row >= col) if reverse else (row <= col)
    bc = jnp.sum(jnp.where(tri, lf_r, 0.0), axis=1, keepdims=True)
    br = jnp.sum(jnp.where(tri_t, lf_c, 0.0), axis=0, keepdims=True)
    dmat = jnp.where(tri, bc - br + ig_r, -jnp.inf)
    m_inter = bc + m
    m_t = jnp.maximum(m_inter, jnp.max(dmat, axis=1, keepdims=True))
    inter = jnp.exp(m_inter - m_t)
    sc = _nt(q, k) * jnp.exp(dmat - m_t)
    cst = c_ref[...]
    num = (jnp.dot(sc.astype(BF16), v, preferred_element_type=F32)
           + inter * jnp.dot(q, cst.astype(BF16), preferred_element_type=F32))
    qn = jnp.sum(q.astype(F32) * n, axis=1, keepdims=True)
    den = jnp.sum(sc, axis=1, keepdims=True) + inter * qn
    h = num / jnp.maximum(jnp.abs(den), jnp.exp(-m_t))

    bend = bc[0:1, :] if reverse else bc[c - 1:c, :]
    w_c = bend - bc + ig_c
    m_new = jnp.maximum(bend + m, jnp.max(w_c, axis=0, keepdims=True))
    carry_scale = jnp.exp(bend + m - m_new)
    kw = k.astype(F32) * jnp.exp(w_c - m_new)
    c_ref[...] = carry_scale * cst + _tn(kw.astype(BF16), v)
    n_new = carry_scale * n + jnp.sum(kw, axis=0, keepdims=True)
    return h, n_new, m_new


def _mlstm_kernel(q_ref, k_ref, v_ref, og_ref, gc_ref, gr_ref, bias_r_ref, bias_c_ref, ng_ref,
                  y_ref, hfw_ref, c_ref, *, chunk):
    t, d = q_ref.shape
    nc = t // chunk
    bias_r = bias_r_ref[...]
    bias_c = bias_c_ref[...]

    def run(cidx, direction, n, m):
        rows = pl.ds(pl.multiple_of(cidx * chunk, chunk), chunk)
        gc = gc_ref[rows, :] + bias_r
        gr = gr_ref[:, rows] + bias_c
        ig_c = gc[:, direction:direction + 1]
        lf_c = _log_sigmoid(gc[:, 2 + direction:3 + direction])
        ig_r = gr[direction:direction + 1, :]
        lf_r = _log_sigmoid(gr[2 + direction:3 + direction, :])
        h, n, m = _mlstm_chunk(q_ref[rows, :], k_ref[rows, :], v_ref[rows, :].astype(BF16),
                               lf_c, ig_c, lf_r, ig_r, c_ref, n, m, reverse=direction == 1)
        return rows, h, n, m

    def init():
        c_ref[...] = jnp.zeros_like(c_ref)
        return jnp.zeros((1, d), F32), jnp.full((1, 1), M_INIT, F32)

    def fw_body(i, carry):
        rows, h, n, m = run(i, 0, *carry)
        hfw_ref[rows, :] = h
        return n, m

    lax.fori_loop(0, nc, fw_body, init())

    ng = ng_ref[...]

    def bw_body(i, carry):
        rows, h, n, m = run(nc - 1 - i, 1, *carry)
        h = h + hfw_ref[rows, :]
        mu = jnp.mean(h, axis=-1, keepdims=True)
        hc = h - mu
        var = jnp.mean(hc * hc, axis=-1, keepdims=True)
        y = hc * lax.rsqrt(var + NORM_EPS) * ng * _sigmoid(og_ref[rows, :])
        y_ref[rows, :] = y.astype(y_ref.dtype)
        return n, m

    lax.fori_loop(0, nc, bw_body, init())


def _mlstm(qk, z, gcol, grow, bias_r, bias_c, norm_g, *, batch, seq, heads, d, v_block0, o_block0,
           chunk=SCAN_CHUNK):
    n = z.shape[0]
    assert n == batch * seq and seq % chunk == 0
    block_bytes = 2 * seq * d * 2 + 2 * seq * d * 4 + 2 * seq * V7X_LANES * 4 + seq * d * 2
    scratch_bytes = seq * d * 4 + d * d * 4
    return pl.pallas_call(
        functools.partial(_mlstm_kernel, chunk=chunk),
        grid=(batch, heads),
        in_specs=[
            pl.BlockSpec((seq, d), lambda b, h: (b, h)),
            pl.BlockSpec((seq, d), lambda b, h: (b, heads + h)),
            pl.BlockSpec((seq, d), lambda b, h: (b, v_block0 + h)),
            pl.BlockSpec((seq, d), lambda b, h: (b, o_block0 + h)),
            pl.BlockSpec((None, None, seq, 4), lambda b, h: (b, h, 0, 0)),
            pl.BlockSpec((None, None, 4, seq), lambda b, h: (b, h, 0, 0)),
            pl.BlockSpec((None, 1, 4), lambda b, h: (h, 0, 0)),
            pl.BlockSpec((None, 4, 1), lambda b, h: (h, 0, 0)),
            pl.BlockSpec((1, d), lambda b, h: (0, h)),
        ],
        out_specs=pl.BlockSpec((seq, d), lambda b, h: (b, h)),
        out_shape=jax.ShapeDtypeStruct((n, heads * d), BF16),
        scratch_shapes=[pltpu.VMEM((seq, d), F32), pltpu.VMEM((d, d), F32)],
        compiler_params=pltpu.CompilerParams(
            dimension_semantics=("parallel", "parallel"),
            vmem_limit_bytes=_vmem_limit(block_bytes, scratch_bytes)),
        name="mlstm",
    )(qk, qk, z, z, gcol, grow, bias_r, bias_c, norm_g)


def _out_proj_kernel(x_ref, yh_ref, ym_ref, wh_ref, wm_ref, g_ref, b_ref, o_ref, *, alpha):
    r = (x_ref[...] * alpha
         + jnp.dot(yh_ref[...], wh_ref[...], preferred_element_type=F32)
         + jnp.dot(ym_ref[...], wm_ref[...], preferred_element_type=F32))
    mu = jnp.mean(r, axis=-1, keepdims=True)
    d = r - mu
    var = jnp.mean(d * d, axis=-1, keepdims=True)
    o_ref[...] = d * lax.rsqrt(var + LN_EPS) * g_ref[...] + b_ref[...]


def _out_proj(x, yh, ym, wh, wm, g, b, *, alpha, tm=512):
    n, d = x.shape
    kh, km = yh.shape[1], ym.shape[1]
    assert n % tm == 0
    block_bytes = tm * d * 4 + tm * kh * 2 + tm * km * 2 + (kh + km) * d * 2 + tm * d * 4
    return pl.pallas_call(
        functools.partial(_out_proj_kernel, alpha=alpha),
        grid=(n // tm,),
        in_specs=[
            pl.BlockSpec((tm, d), lambda i: (i, 0)),
            pl.BlockSpec((tm, kh), lambda i: (i, 0)),
            pl.BlockSpec((tm, km), lambda i: (i, 0)),
            pl.BlockSpec((kh, d), lambda i: (0, 0)),
            pl.BlockSpec((km, d), lambda i: (0, 0)),
            pl.BlockSpec((1, d), lambda i: (0, 0)),
            pl.BlockSpec((1, d), lambda i: (0, 0)),
        ],
        out_specs=pl.BlockSpec((tm, d), lambda i: (i, 0)),
        out_shape=jax.ShapeDtypeStruct((n, d), F32),
        compiler_params=pltpu.CompilerParams(
            dimension_semantics=("parallel",),
            vmem_limit_bytes=_vmem_limit(block_bytes, 0)),
        name="out_proj",
    )(x, yh, ym, wh, wm, g, b)


def kernel(x, ffn1_w1, ffn1_w3, ffn1_w2, ln1_g, ln1_b, w_in, hgrn_lb, hgrn_norm_g, mlstm_conv_w,
           mlstm_conv_b, mlstm_ig_b, mlstm_fg_b, mlstm_norm_g, w_out, ln2_g, ln2_b, ffn2_w1, ffn2_w3,
           ffn2_w2, ln3_g, ln3_b):
    batch, seq, d_model = x.shape
    depth = ffn1_w1.shape[0]
    assert depth == 1 and hgrn_lb.shape[1] == 2
    alpha = (2.0 * depth) ** 0.25
    hgrn_width = hgrn_norm_g.shape[1]
    mlstm_width = mlstm_norm_g.shape[1]
    hgrn_heads = hgrn_width // HGRN_HEAD_DIM
    md = mlstm_width // MLSTM_HEADS
    main_cols = 5 * hgrn_width + 4 * mlstm_width
    n = batch * seq

    def row(p):
        return p.reshape(1, -1).astype(F32)

    xf = x.reshape(n, d_model)
    x1, x1bf = _ffn_ln(xf, ffn1_w1[0].astype(BF16), ffn1_w3[0].astype(BF16), ffn1_w2[0].astype(BF16),
                       row(ln1_g[0]), row(ln1_b[0]), alpha=alpha, emit_bf16=True)

    w_in_bf = w_in[0].astype(BF16)
    n_gates = w_in_bf.shape[1] - main_cols
    w_gates = jnp.pad(w_in_bf[:, main_cols:], ((0, 0), (0, V7X_LANES - n_gates)))
    z, gates = _in_proj(x1bf, w_in_bf[:, :main_cols], w_gates)

    y_h = _hgrn(z, hgrn_lb.reshape(4, hgrn_width).astype(F32), row(hgrn_norm_g[0]),
                batch=batch, seq=seq, heads=hgrn_heads)

    qk = _qk_conv(z, mlstm_conv_w[0].astype(F32), row(mlstm_conv_b[0]), batch=batch, seq=seq,
                  col_block0=5 * hgrn_width // 256, width=2 * mlstm_width, q_width=mlstm_width,
                  q_scale=md ** -0.5)

    g4 = gates[:, :n_gates].reshape(batch, seq, 4, MLSTM_HEADS)
    gcol = g4.transpose(0, 3, 1, 2)
    grow = g4.transpose(0, 3, 2, 1)
    bias4 = jnp.stack([mlstm_ig_b[0, 0], mlstm_ig_b[0, 1], mlstm_fg_b[0, 0], mlstm_fg_b[0, 1]],
                      axis=-1).astype(F32)
    y_m = _mlstm(qk, z, gcol, grow, bias4[:, None, :], bias4[:, :, None], row(mlstm_norm_g[0]),
                 batch=batch, seq=seq, heads=MLSTM_HEADS, d=md,
                 v_block0=(5 * hgrn_width + 2 * mlstm_width) // md,
                 o_block0=(5 * hgrn_width + 3 * mlstm_width) // md)

    w_out_bf = w_out[0].astype(BF16)
    x2 = _out_proj(x1, y_h, y_m, w_out_bf[:hgrn_width], w_out_bf[hgrn_width:],
                      row(ln2_g[0]), row(ln2_b[0]), alpha=alpha)

    (x3,) = _ffn_ln(x2, ffn2_w1[0].astype(BF16), ffn2_w3[0].astype(BF16), ffn2_w2[0].astype(BF16),
                    row(ln3_g[0]), row(ln3_b[0]), alpha=alpha, emit_bf16=False)
    return x3.reshape(batch, seq, d_model)
```

```python
import functools

import jax
import jax.numpy as jnp
from jax import lax
from jax.experimental import pallas as pl
from jax.experimental.pallas import tpu as pltpu

F32 = jnp.float32
BF16 = jnp.bfloat16

V7X_VMEM_BYTES = 64 * 1024 * 1024
V7X_LANES = 128
V7X_SUBLANES = 8

HGRN_HEAD_DIM = 128
MLSTM_HEADS = 4
CONV_WIDTH = 5
LN_EPS = 1e-5
NORM_EPS = 1e-6
M_INIT = -1e30
SCAN_CHUNK = 128


def _vmem_limit(block_bytes, scratch_bytes):
    need = 2 * block_bytes + scratch_bytes
    return int(min(V7X_VMEM_BYTES - (4 << 20), need + (16 << 20)))


def _nt(a, b):
    return lax.dot_general(a, b, (((1,), (1,)), ((), ())), preferred_element_type=F32)


def _tn(a, b):
    return lax.dot_general(a, b, (((0,), (0,)), ((), ())), preferred_element_type=F32)


def _sigmoid(x):
    return 1.0 / (1.0 + jnp.exp(-x))


def _log_sigmoid(x):
    return -(jnp.maximum(-x, 0.0) + jnp.log(1.0 + jnp.exp(-jnp.abs(x))))


def _ffn_ln_kernel(x_ref, w1_ref, w3_ref, w2_ref, g_ref, b_ref, *rest, alpha, emit_bf16):
    if emit_bf16:
        o_ref, obf_ref, xbf_ref, acc_ref = rest
    else:
        o_ref, xbf_ref, acc_ref = rest
    j = pl.program_id(1)

    @pl.when(j == 0)
    def _():
        xbf_ref[...] = x_ref[...].astype(BF16)
        acc_ref[...] = jnp.zeros_like(acc_ref)

    xb = xbf_ref[...]
    h1 = jnp.dot(xb, w1_ref[...], preferred_element_type=F32)
    h3 = jnp.dot(xb, w3_ref[...], preferred_element_type=F32)
    gate = (h1 * _sigmoid(h1) * h3).astype(BF16)
    acc_ref[...] += jnp.dot(gate, w2_ref[...], preferred_element_type=F32)

    @pl.when(j == pl.num_programs(1) - 1)
    def _():
        r = x_ref[...] * alpha + 0.5 * acc_ref[...]
        mu = jnp.mean(r, axis=-1, keepdims=True)
        d = r - mu
        var = jnp.mean(d * d, axis=-1, keepdims=True)
        y = d * lax.rsqrt(var + LN_EPS) * g_ref[...] + b_ref[...]
        o_ref[...] = y
        if emit_bf16:
            obf_ref[...] = y.astype(BF16)


def _ffn_ln(x, w1, w3, w2, g, b, *, alpha, emit_bf16, tm=512, tf=512):
    n, d = x.shape
    f = w1.shape[1]
    assert n % tm == 0 and f % tf == 0
    out_shape = [jax.ShapeDtypeStruct((n, d), F32)]
    out_specs = [pl.BlockSpec((tm, d), lambda i, j: (i, 0))]
    if emit_bf16:
        out_shape.append(jax.ShapeDtypeStruct((n, d), BF16))
        out_specs.append(pl.BlockSpec((tm, d), lambda i, j: (i, 0)))
    block_bytes = tm * d * 4 + 2 * d * tf * 2 + tf * d * 2 + tm * d * 4 + (tm * d * 2 if emit_bf16 else 0)
    scratch_bytes = tm * d * 2 + tm * d * 4
    return pl.pallas_call(
        functools.partial(_ffn_ln_kernel, alpha=alpha, emit_bf16=emit_bf16),
        grid=(n // tm, f // tf),
        in_specs=[
            pl.BlockSpec((tm, d), lambda i, j: (i, 0)),
            pl.BlockSpec((d, tf), lambda i, j: (0, j)),
            pl.BlockSpec((d, tf), lambda i, j: (0, j)),
            pl.BlockSpec((tf, d), lambda i, j: (j, 0)),
            pl.BlockSpec((1, d), lambda i, j: (0, 0)),
            pl.BlockSpec((1, d), lambda i, j: (0, 0)),
        ],
        out_specs=out_specs,
        out_shape=out_shape,
        scratch_shapes=[pltpu.VMEM((tm, d), BF16), pltpu.VMEM((tm, d), F32)],
        compiler_params=pltpu.CompilerParams(
            dimension_semantics=("parallel", "arbitrary"),
            vmem_limit_bytes=_vmem_limit(block_bytes, scratch_bytes)),
        name="ffn_ln",
    )(x, w1, w3, w2, g, b)


def _in_proj_kernel(x_ref, w_ref, wg_ref, z_ref, gates_ref):
    z_ref[...] = jnp.dot(x_ref[...], w_ref[...], preferred_element_type=F32)

    @pl.when(pl.program_id(1) == 0)
    def _():
        gates_ref[...] = jnp.dot(x_ref[...], wg_ref[...], preferred_element_type=F32)


def _in_proj(xbf, w_all, w_gates, *, cols, tm=1024, tn=1024):
    n, d = xbf.shape
    gl = w_gates.shape[1]
    assert n % tm == 0 and cols % tn == 0
    block_bytes = tm * d * 2 + d * tn * 2 + d * gl * 2 + tm * tn * 4 + tm * gl * 4
    return pl.pallas_call(
        _in_proj_kernel,
        grid=(n // tm, cols // tn),
        in_specs=[
            pl.BlockSpec((tm, d), lambda i, j: (i, 0)),
            pl.BlockSpec((d, tn), lambda i, j: (0, j)),
            pl.BlockSpec((d, gl), lambda i, j: (0, 0)),
        ],
        out_specs=[
            pl.BlockSpec((tm, tn), lambda i, j: (i, j)),
            pl.BlockSpec((tm, gl), lambda i, j: (i, 0)),
        ],
        out_shape=[jax.ShapeDtypeStruct((n, cols), F32), jax.ShapeDtypeStruct((n, gl), F32)],
        compiler_params=pltpu.CompilerParams(
            dimension_semantics=("parallel", "arbitrary"),
            vmem_limit_bytes=_vmem_limit(block_bytes, 0)),
        name="in_proj",
    )(xbf, w_all, w_gates)


LOG2E = 1.4426950408889634


def _cumsum01(tri_bf, g):
    g1 = g.astype(BF16)
    r1 = g - g1.astype(F32)
    g2 = r1.astype(BF16)
    g3 = (r1 - g2.astype(F32)).astype(BF16)
    return (jnp.dot(tri_bf, g1, preferred_element_type=F32)
            + jnp.dot(tri_bf, g2, preferred_element_type=F32)
            + jnp.dot(tri_bf, g3, preferred_element_type=F32))


def _neg_abs(x):
    bits = lax.bitcast_convert_type(x, jnp.uint32) | jnp.uint32(0x80000000)
    return lax.bitcast_convert_type(bits, F32)


def _hgrn_level_operand(q, k, f, b, m, *, reverse):
    c, d = q.shape
    sub = V7X_SUBLANES
    if m >= sub:
        refs, sides = [], []
        for p in range(c // (2 * m)):
            r = p * 2 * m + (m if reverse else m - 1)
            refs.append(jnp.broadcast_to(b[r:r + 1, :], (2 * m, d)))
            lo, hi = slice(p * 2 * m, p * 2 * m + m), slice(p * 2 * m + m, (p + 1) * 2 * m)
            sides += [q[lo], k[hi]] if reverse else [k[lo], q[hi]]
        ref = refs[0] if len(refs) == 1 else jnp.concatenate(refs, axis=0)
        y0 = jnp.concatenate(sides, axis=0)
        return (y0 * jnp.exp2(_neg_abs(b - ref))).astype(BF16)
    nb = c // sub
    s_idx = lax.broadcasted_iota(jnp.int32, (1, sub, d), 1)
    q_side = ((s_idx & m) == 0) if reverse else ((s_idx & m) != 0)
    q3, k3 = q.reshape(nb, sub, d), k.reshape(nb, sub, d)
    if m == 1:
        y3 = jnp.where(q_side, q3 * f.reshape(nb, sub, d), k3)
        return y3.reshape(c, d).astype(BF16)
    b3 = b.reshape(nb, sub, d)
    r0 = m if reverse else m - 1
    ref3 = jnp.broadcast_to(b3[:, r0:r0 + 1, :], (nb, sub, d))
    for r in range(r0 + 2 * m, sub, 2 * m):
        ref3 = jnp.where(s_idx < r - r0, ref3, jnp.broadcast_to(b3[:, r:r + 1, :], (nb, sub, d)))
    y3 = jnp.where(q_side, q3, k3) * jnp.exp2(_neg_abs(b3 - ref3))
    return y3.reshape(c, d).astype(BF16)


def _hgrn_chunk(q, k, v, f, b, st, level_map, *, reverse):
    c, d = q.shape
    bend = b[0:1, :] if reverse else b[c - 1:c, :]
    qd = (q * jnp.exp2(b)).astype(BF16)
    kd = (k * jnp.exp2(bend - b)).astype(BF16)
    vb = v.astype(BF16)
    o = _nt(qd, st.astype(BF16))
    st_new = st * jnp.exp2(bend) + _tn(vb, kd)

    attn = jnp.zeros((c, c), F32)
    m, level = 1, 0
    while m < c:
        y = _hgrn_level_operand(q, k, f, b, m, reverse=reverse)
        attn = jnp.where(level_map == level, _nt(y, y), attn)
        m, level = 2 * m, level + 1
    o = o + jnp.dot(attn.astype(BF16), vb, preferred_element_type=F32)
    o = o + jnp.sum(q * k, axis=-1, keepdims=True) * v
    return o, st_new


def _hgrn_kernel(zq_ref, zi_ref, zg_ref, zff_ref, zfb_ref, lbp_ref, ng_ref, y_ref,
                 q_ref, f_ref, b_ref, o_ref, st_ref, *, chunk):
    t, d = zq_ref.shape
    nc = t // chunk
    lbp = lbp_ref[...]

    def lower_bound(a0, a1):
        mx = jnp.maximum(a0, a1)
        e0 = jnp.exp(a0 - mx)
        e1 = jnp.exp(a1 - mx)
        return e0 / (e0 + e1)

    lbs = (lower_bound(lbp[0:1, :], lbp[1:2, :]), lower_bound(lbp[2:3, :], lbp[3:4, :]))
    zf_refs = (zff_ref, zfb_ref)
    scale = HGRN_HEAD_DIM ** -0.5

    row = lax.broadcasted_iota(jnp.int32, (chunk, chunk), 0)
    col = lax.broadcasted_iota(jnp.int32, (chunk, chunk), 1)

    def chunk_rows(cidx):
        return pl.ds(pl.multiple_of(cidx * chunk, chunk), chunk)

    tris = (jnp.where(col <= row, 1.0, 0.0).astype(BF16), jnp.where(col >= row, 1.0, 0.0).astype(BF16))

    def prepare(cidx, carry):
        rows = chunk_rows(cidx)
        zq = zq_ref[rows, :]
        q_ref[rows, :] = zq * _sigmoid(zq) * scale
        o_ref[rows, :] = jnp.zeros((chunk, d), F32)
        for direction in (0, 1):
            lb = lbs[direction]
            f = lb + (1.0 - lb) * _sigmoid(zf_refs[direction][rows, :])
            f_ref[direction, rows, :] = f
            b_ref[direction, rows, :] = _cumsum01(tris[direction], jnp.log(f) * LOG2E)
        return carry

    lax.fori_loop(0, nc, prepare, 0, unroll=4)

    differ = row ^ col
    top_bit = jnp.zeros((chunk, chunk), jnp.int32)
    m = 2
    while m < chunk:
        top_bit = top_bit + jnp.where(differ >= m, 1, 0)
        m *= 2
    level_maps = (jnp.where(row > col, top_bit, -1), jnp.where(col > row, top_bit, -1))
    st_ref[...] = jnp.zeros_like(st_ref)

    def scan(i, carry):
        for direction in (0, 1):
            rows = chunk_rows(i if direction == 0 else nc - 1 - i)
            f = f_ref[direction, rows, :]
            o, st_new = _hgrn_chunk(q_ref[rows, :], 1.0 - f, zi_ref[rows, :], f, b_ref[direction, rows, :],
                                    st_ref[direction], level_maps[direction], reverse=direction == 1)
            st_ref[direction] = st_new
            o_ref[rows, :] += o
        return carry

    lax.fori_loop(0, nc, scan, 0, unroll=2)

    ng = ng_ref[...]

    def finish(cidx, carry):
        rows = chunk_rows(cidx)
        o = o_ref[rows, :]
        o = o * lax.rsqrt(jnp.mean(o * o, axis=-1, keepdims=True) + NORM_EPS)
        zg = zg_ref[rows, :]
        y_ref[rows, :] = (o * ng * (zg * _sigmoid(zg))).astype(y_ref.dtype)
        return carry

    lax.fori_loop(0, nc, finish, 0, unroll=4)


def _hgrn(z, lbp, norm_g, *, batch, seq, heads, chunk=SCAN_CHUNK):
    d = HGRN_HEAD_DIM
    n = z.shape[0]
    assert n == batch * seq and seq % (2 * chunk) == 0

    def zspec(off):
        return pl.BlockSpec((seq, d), lambda b, h: (b, off * heads + h))

    block_bytes = 5 * seq * d * 4 + seq * d * 2
    scratch_bytes = 6 * seq * d * 4 + 2 * d * d * 4
    return pl.pallas_call(
        functools.partial(_hgrn_kernel, chunk=chunk),
        grid=(batch, heads),
        in_specs=[zspec(0), zspec(1), zspec(2), zspec(3), zspec(4),
                  pl.BlockSpec((4, d), lambda b, h: (0, h)),
                  pl.BlockSpec((1, d), lambda b, h: (0, h))],
        out_specs=pl.BlockSpec((seq, d), lambda b, h: (b, h)),
        out_shape=jax.ShapeDtypeStruct((n, heads * d), BF16),
        scratch_shapes=[pltpu.VMEM((seq, d), F32), pltpu.VMEM((2, seq, d), F32), pltpu.VMEM((2, seq, d), F32),
                        pltpu.VMEM((seq, d), F32), pltpu.VMEM((2, d, d), F32)],
        compiler_params=pltpu.CompilerParams(
            dimension_semantics=("parallel", "parallel"),
            vmem_limit_bytes=_vmem_limit(block_bytes, scratch_bytes)),
        name="hgrn",
    )(z, z, z, z, z, lbp, norm_g)


def _qk_conv_kernel(x_ref, w_ref, b_ref, o_ref, xp_ref, *, tile, q_blocks, q_scale):
    t, c = x_ref.shape
    pad = V7X_SUBLANES
    half = CONV_WIDTH // 2
    xp_ref[0:pad, :] = jnp.zeros((pad, c), F32)
    xp_ref[pad + t:2 * pad + t, :] = jnp.zeros((pad, c), F32)
    xp_ref[pad:pad + t, :] = x_ref[...]
    w = w_ref[...]
    bias = b_ref[...]
    scale = jnp.where(pl.program_id(1) < q_blocks, q_scale, 1.0)
    win = tile + 2 * pad

    def body(i, carry):
        r0 = pl.multiple_of(i * tile, tile)
        window = xp_ref[pl.ds(r0, win), :]
        acc = bias + w[half:half + 1, :] * window[pad:pad + tile, :]
        for j in range(CONV_WIDTH):
            off = j - half
            if off == 0:
                continue
            shifted = pltpu.roll(window, (win - off) % win, 0)
            acc = acc + w[j:j + 1, :] * shifted[pad:pad + tile, :]
        o_ref[pl.ds(r0, tile), :] = (acc * _sigmoid(acc) * scale).astype(o_ref.dtype)
        return carry

    lax.fori_loop(0, t // tile, body, 0)


def _qk_conv(z, conv_w, conv_b, *, batch, seq, col_block0, width, q_width, q_scale, cblk=256, tile=256):
    n = z.shape[0]
    assert seq % tile == 0 and width % cblk == 0 and q_width % cblk == 0
    block_bytes = seq * cblk * 4 + CONV_WIDTH * cblk * 4 + cblk * 4 + seq * cblk * 2
    scratch_bytes = (seq + 2 * V7X_SUBLANES) * cblk * 4
    return pl.pallas_call(
        functools.partial(_qk_conv_kernel, tile=tile, q_blocks=q_width // cblk, q_scale=q_scale),
        grid=(batch, width // cblk),
        in_specs=[pl.BlockSpec((seq, cblk), lambda b, j: (b, col_block0 + j)),
                  pl.BlockSpec((CONV_WIDTH, cblk), lambda b, j: (0, j)),
                  pl.BlockSpec((1, cblk), lambda b, j: (0, j))],
        out_specs=pl.BlockSpec((seq, cblk), lambda b, j: (b, j)),
        out_shape=jax.ShapeDtypeStruct((n, width), BF16),
        scratch_shapes=[pltpu.VMEM((seq + 2 * V7X_SUBLANES, cblk), F32)],
        compiler_params=pltpu.CompilerParams(
            dimension_semantics=("parallel", "parallel"),
            vmem_limit_bytes=_vmem_limit(block_bytes, scratch_bytes)),
        name="qk_conv",
    )(z, conv_w, conv_b)


def _mlstm_chunk(q, k, v, lf_c, ig_c, lf_r, ig_r, c_ref, n, m, *, reverse):
    c = q.shape[0]
    row = lax.broadcasted_iota(jnp.int32, (c, c), 0)
    col = lax.broadcasted_iota(jnp.int32, (c, c), 1)
    tri = (col >= row) if reverse else (col <= row)
    tri_t = (row >= col) if reverse else (row <= col)
    bc = jnp.sum(jnp.where(tri, lf_r, 0.0), axis=1, keepdims=True)
    br = jnp.sum(jnp.where(tri_t, lf_c, 0.0), axis=0, keepdims=True)
    dmat = jnp.where(tri, bc - br + ig_r, -jnp.inf)
    m_inter = bc + m
    m_t = jnp.maximum(m_inter, jnp.max(dmat, axis=1, keepdims=True))
    inter = jnp.exp(m_inter - m_t)
    sc = _nt(q, k) * jnp.exp(dmat - m_t)
    cst = c_ref[...]
    num = (jnp.dot(sc.astype(BF16), v, preferred_element_type=F32)
           + inter * jnp.dot(q, cst.astype(BF16), preferred_element_type=F32))
    qn = jnp.sum(q.astype(F32) * n, axis=1, keepdims=True)
    den = jnp.sum(sc, axis=1, keepdims=True) + inter * qn
    h = num / jnp.maximum(jnp.abs(den), jnp.exp(-m_t))

    bend = bc[0:1, :] if reverse else bc[c - 1:c, :]
    w_c = bend - bc + ig_c
    m_new = jnp.maximum(bend + m, jnp.max(w_c, axis=0, keepdims=True))
    carry_scale = jnp.exp(bend + m - m_new)
    kw = k.astype(F32) * jnp.exp(w_c - m_new)
    c_ref[...] = carry_scale * cst + _tn(kw.astype(BF16), v)
    n_new = carry_scale * n + jnp.sum(kw, axis=0, keepdims=True)
    return h, n_new, m_new


def _mlstm_kernel(q_ref, k_ref, v_ref, og_ref, gc_ref, gr_ref, bias_r_ref, bias_c_ref, ng_ref,
                  y_ref, h_ref, c_ref, *, chunk):
    t, d = q_ref.shape
    nc = t // chunk
    bias_r = bias_r_ref[...]
    bias_c = bias_c_ref[...]

    def chunk_rows(cidx):
        return pl.ds(pl.multiple_of(cidx * chunk, chunk), chunk)

    def clear(cidx, carry):
        h_ref[chunk_rows(cidx), :] = jnp.zeros((chunk, d), F32)
        return carry

    lax.fori_loop(0, nc, clear, 0, unroll=4)
    c_ref[...] = jnp.zeros_like(c_ref)

    def scan(i, carry):
        new_carry = []
        for direction in (0, 1):
            n, m = carry[direction]
            rows = chunk_rows(i if direction == 0 else nc - 1 - i)
            gc = gc_ref[rows, :] + bias_r
            gr = gr_ref[:, rows] + bias_c
            ig_c = gc[:, direction:direction + 1]
            lf_c = _log_sigmoid(gc[:, 2 + direction:3 + direction])
            ig_r = gr[direction:direction + 1, :]
            lf_r = _log_sigmoid(gr[2 + direction:3 + direction, :])
            h, n, m = _mlstm_chunk(q_ref[rows, :], k_ref[rows, :], v_ref[rows, :].astype(BF16),
                                   lf_c, ig_c, lf_r, ig_r, c_ref.at[direction], n, m,
                                   reverse=direction == 1)
            h_ref[rows, :] += h
            new_carry.append((n, m))
        return tuple(new_carry)

    init = (jnp.zeros((1, d), F32), jnp.full((1, 1), M_INIT, F32))
    lax.fori_loop(0, nc, scan, (init, init), unroll=2)

    ng = ng_ref[...]

    def finish(cidx, carry):
        rows = chunk_rows(cidx)
        h = h_ref[rows, :]
        mu = jnp.mean(h, axis=-1, keepdims=True)
        hc = h - mu
        var = jnp.mean(hc * hc, axis=-1, keepdims=True)
        y = hc * lax.rsqrt(var + NORM_EPS) * ng * _sigmoid(og_ref[rows, :])
        y_ref[rows, :] = y.astype(y_ref.dtype)
        return carry

    lax.fori_loop(0, nc, finish, 0, unroll=4)


def _mlstm(qk, z, gcol, grow, bias_r, bias_c, norm_g, *, batch, seq, heads, d, v_block0, o_block0,
           chunk=SCAN_CHUNK):
    n = z.shape[0]
    assert n == batch * seq and seq % (2 * chunk) == 0
    block_bytes = 2 * seq * d * 2 + 2 * seq * d * 4 + 2 * seq * V7X_LANES * 4 + seq * d * 2
    scratch_bytes = seq * d * 4 + 2 * d * d * 4
    return pl.pallas_call(
        functools.partial(_mlstm_kernel, chunk=chunk),
        grid=(batch, heads),
        in_specs=[
            pl.BlockSpec((seq, d), lambda b, h: (b, h)),
            pl.BlockSpec((seq, d), lambda b, h: (b, heads + h)),
            pl.BlockSpec((seq, d), lambda b, h: (b, v_block0 + h)),
            pl.BlockSpec((seq, d), lambda b, h: (b, o_block0 + h)),
            pl.BlockSpec((None, None, seq, 4), lambda b, h: (b, h, 0, 0)),
            pl.BlockSpec((None, None, 4, seq), lambda b, h: (b, h, 0, 0)),
            pl.BlockSpec((None, 1, 4), lambda b, h: (h, 0, 0)),
            pl.BlockSpec((None, 4, 1), lambda b, h: (h, 0, 0)),
            pl.BlockSpec((1, d), lambda b, h: (0, h)),
        ],
        out_specs=pl.BlockSpec((seq, d), lambda b, h: (b, h)),
        out_shape=jax.ShapeDtypeStruct((n, heads * d), BF16),
        scratch_shapes=[pltpu.VMEM((seq, d), F32), pltpu.VMEM((2, d, d), F32)],
        compiler_params=pltpu.CompilerParams(
            dimension_semantics=("parallel", "parallel"),
            vmem_limit_bytes=_vmem_limit(block_bytes, scratch_bytes)),
        name="mlstm",
    )(qk, qk, z, z, gcol, grow, bias_r, bias_c, norm_g)


def _out_proj_kernel(x_ref, yh_ref, ym_ref, wh_ref, wm_ref, g_ref, b_ref, o_ref, *, alpha):
    r = (x_ref[...] * alpha
         + jnp.dot(yh_ref[...], wh_ref[...], preferred_element_type=F32)
         + jnp.dot(ym_ref[...], wm_ref[...], preferred_element_type=F32))
    mu = jnp.mean(r, axis=-1, keepdims=True)
    d = r - mu
    var = jnp.mean(d * d, axis=-1, keepdims=True)
    o_ref[...] = d * lax.rsqrt(var + LN_EPS) * g_ref[...] + b_ref[...]


def _out_proj(x, yh, ym, wh, wm, g, b, *, alpha, tm=512):
    n, d = x.shape
    kh, km = yh.shape[1], ym.shape[1]
    assert n % tm == 0
    block_bytes = tm * d * 4 + tm * kh * 2 + tm * km * 2 + (kh + km) * d * 2 + tm * d * 4
    return pl.pallas_call(
        functools.partial(_out_proj_kernel, alpha=alpha),
        grid=(n // tm,),
        in_specs=[
            pl.BlockSpec((tm, d), lambda i: (i, 0)),
            pl.BlockSpec((tm, kh), lambda i: (i, 0)),
            pl.BlockSpec((tm, km), lambda i: (i, 0)),
            pl.BlockSpec((kh, d), lambda i: (0, 0)),
            pl.BlockSpec((km, d), lambda i: (0, 0)),
            pl.BlockSpec((1, d), lambda i: (0, 0)),
            pl.BlockSpec((1, d), lambda i: (0, 0)),
        ],
        out_specs=pl.BlockSpec((tm, d), lambda i: (i, 0)),
        out_shape=jax.ShapeDtypeStruct((n, d), F32),
        compiler_params=pltpu.CompilerParams(
            dimension_semantics=("parallel",),
            vmem_limit_bytes=_vmem_limit(block_bytes, 0)),
        name="out_proj",
    )(x, yh, ym, wh, wm, g, b)


def kernel(x, ffn1_w1, ffn1_w3, ffn1_w2, ln1_g, ln1_b, w_in, hgrn_lb, hgrn_norm_g, mlstm_conv_w,
           mlstm_conv_b, mlstm_ig_b, mlstm_fg_b, mlstm_norm_g, w_out, ln2_g, ln2_b, ffn2_w1, ffn2_w3,
           ffn2_w2, ln3_g, ln3_b):
    batch, seq, d_model = x.shape
    depth = ffn1_w1.shape[0]
    assert depth == 1 and hgrn_lb.shape[1] == 2
    alpha = (2.0 * depth) ** 0.25
    hgrn_width = hgrn_norm_g.shape[1]
    mlstm_width = mlstm_norm_g.shape[1]
    hgrn_heads = hgrn_width // HGRN_HEAD_DIM
    md = mlstm_width // MLSTM_HEADS
    main_cols = 5 * hgrn_width + 4 * mlstm_width
    n = batch * seq

    def row(p):
        return p.reshape(1, -1).astype(F32)

    xf = x.reshape(n, d_model)
    x1, x1bf = _ffn_ln(xf, ffn1_w1[0].astype(BF16), ffn1_w3[0].astype(BF16), ffn1_w2[0].astype(BF16),
                       row(ln1_g[0]), row(ln1_b[0]), alpha=alpha, emit_bf16=True)

    n_gates = w_in.shape[2] - main_cols
    w_gates = jnp.pad(w_in[0, :, main_cols:].astype(BF16), ((0, 0), (0, V7X_LANES - n_gates)))
    z, gates = _in_proj(x1bf, w_in[0].astype(BF16), w_gates, cols=main_cols)

    y_h = _hgrn(z, hgrn_lb.reshape(4, hgrn_width).astype(F32), row(hgrn_norm_g[0]),
                batch=batch, seq=seq, heads=hgrn_heads)

    qk = _qk_conv(z, mlstm_conv_w[0].astype(F32), row(mlstm_conv_b[0]), batch=batch, seq=seq,
                  col_block0=5 * hgrn_width // 256, width=2 * mlstm_width, q_width=mlstm_width,
                  q_scale=md ** -0.5)

    g4 = gates[:, :n_gates].reshape(batch, seq, 4, MLSTM_HEADS)
    gcol = g4.transpose(0, 3, 1, 2)
    grow = g4.transpose(0, 3, 2, 1)
    bias4 = jnp.stack([mlstm_ig_b[0, 0], mlstm_ig_b[0, 1], mlstm_fg_b[0, 0], mlstm_fg_b[0, 1]],
                      axis=-1).astype(F32)
    y_m = _mlstm(qk, z, gcol, grow, bias4[:, None, :], bias4[:, :, None], row(mlstm_norm_g[0]),
                 batch=batch, seq=seq, heads=MLSTM_HEADS, d=md,
                 v_block0=(5 * hgrn_width + 2 * mlstm_width) // md,
                 o_block0=(5 * hgrn_width + 3 * mlstm_width) // md)

    w_out_bf = w_out[0].astype(BF16)
    x2 = _out_proj(x1, y_h, y_m, w_out_bf[:hgrn_width], w_out_bf[hgrn_width:],
                   row(ln2_g[0]), row(ln2_b[0]), alpha=alpha)

    (x3,) = _ffn_ln(x2, ffn2_w1[0].astype(BF16), ffn2_w3[0].astype(BF16), ffn2_w2[0].astype(BF16),
                    row(ln3_g[0]), row(ln3_b[0]), alpha=alpha, emit_bf16=False)
    return x3.reshape(batch, seq, d_model)
```

```python
import functools

import jax
import jax.numpy as jnp
from jax import lax
from jax.experimental import pallas as pl
from jax.experimental.pallas import tpu as pltpu

F32 = jnp.float32
BF16 = jnp.bfloat16

V7X_VMEM_BYTES = 64 * 1024 * 1024
V7X_LANES = 128
V7X_SUBLANES = 8

HGRN_HEAD_DIM = 128
MLSTM_HEADS = 4
CONV_WIDTH = 5
LN_EPS = 1e-5
NORM_EPS = 1e-6
M_INIT = -1e30
SCAN_CHUNK = 128


def _vmem_limit(block_bytes, scratch_bytes):
    need = 2 * block_bytes + scratch_bytes
    return int(min(V7X_VMEM_BYTES - (4 << 20), need + (16 << 20)))


def _nt(a, b):
    return lax.dot_general(a, b, (((1,), (1,)), ((), ())), preferred_element_type=F32)


def _tn(a, b):
    return lax.dot_general(a, b, (((0,), (0,)), ((), ())), preferred_element_type=F32)


def _sigmoid(x):
    return 1.0 / (1.0 + jnp.exp(-x))


def _log_sigmoid(x):
    return -(jnp.maximum(-x, 0.0) + jnp.log(1.0 + jnp.exp(-jnp.abs(x))))


def _ffn_ln_kernel(x_ref, w1_ref, w3_ref, w2_ref, g_ref, b_ref, o_ref, xbf_ref, *, alpha):
    j = pl.program_id(1)

    @pl.when(j == 0)
    def _():
        xbf_ref[...] = x_ref[...].astype(BF16)
        o_ref[...] = jnp.zeros_like(o_ref)

    xb = xbf_ref[...]
    h1 = jnp.dot(xb, w1_ref[...].astype(BF16), preferred_element_type=F32)
    h3 = jnp.dot(xb, w3_ref[...].astype(BF16), preferred_element_type=F32)
    gate = (h1 * _sigmoid(h1) * h3).astype(BF16)
    o_ref[...] += jnp.dot(gate, w2_ref[...].astype(BF16), preferred_element_type=F32)

    @pl.when(j == pl.num_programs(1) - 1)
    def _():
        r = x_ref[...] * alpha + 0.5 * o_ref[...]
        mu = jnp.mean(r, axis=-1, keepdims=True)
        d = r - mu
        var = jnp.mean(d * d, axis=-1, keepdims=True)
        o_ref[...] = d * lax.rsqrt(var + LN_EPS) * g_ref[...] + b_ref[...]


def _ffn_ln(x, w1, w3, w2, g, b, *, alpha, tm=1024, tf=256):
    n, d = x.shape
    f = w1.shape[1]
    assert n % tm == 0 and f % tf == 0
    block_bytes = tm * d * 4 + 3 * d * tf * 4 + tm * d * 4
    scratch_bytes = tm * d * 2 + 3 * d * tf * 2 + 3 * tm * tf * 4
    return pl.pallas_call(
        functools.partial(_ffn_ln_kernel, alpha=alpha),
        grid=(n // tm, f // tf),
        in_specs=[
            pl.BlockSpec((tm, d), lambda i, j: (i, 0)),
            pl.BlockSpec((d, tf), lambda i, j: (0, j)),
            pl.BlockSpec((d, tf), lambda i, j: (0, j)),
            pl.BlockSpec((tf, d), lambda i, j: (j, 0)),
            pl.BlockSpec((1, d), lambda i, j: (0, 0)),
            pl.BlockSpec((1, d), lambda i, j: (0, 0)),
        ],
        out_specs=pl.BlockSpec((tm, d), lambda i, j: (i, 0)),
        out_shape=jax.ShapeDtypeStruct((n, d), F32),
        scratch_shapes=[pltpu.VMEM((tm, d), BF16)],
        compiler_params=pltpu.CompilerParams(
            dimension_semantics=("parallel", "arbitrary"),
            vmem_limit_bytes=_vmem_limit(block_bytes, scratch_bytes)),
        name="ffn_ln",
    )(x, w1, w3, w2, g, b)


def _in_proj_kernel(x_ref, w_ref, wg_ref, z_ref, gates_ref, xbf_ref):
    @pl.when(pl.program_id(1) == 0)
    def _():
        xbf_ref[...] = x_ref[...].astype(BF16)
        gates_ref[...] = jnp.dot(xbf_ref[...], wg_ref[...].astype(BF16), preferred_element_type=F32)

    z_ref[...] = jnp.dot(xbf_ref[...], w_ref[...].astype(BF16), preferred_element_type=F32)


def _in_proj(x, w_all, w_gates, *, cols, tm=1024, tn=1024):
    n, d = x.shape
    gl = w_gates.shape[1]
    assert n % tm == 0 and cols % tn == 0
    block_bytes = tm * d * 4 + d * tn * 4 + d * gl * 4 + tm * tn * 4 + tm * gl * 4
    scratch_bytes = tm * d * 2 + d * tn * 2
    return pl.pallas_call(
        _in_proj_kernel,
        grid=(n // tm, cols // tn),
        in_specs=[
            pl.BlockSpec((tm, d), lambda i, j: (i, 0)),
            pl.BlockSpec((d, tn), lambda i, j: (0, j)),
            pl.BlockSpec((d, gl), lambda i, j: (0, 0)),
        ],
        out_specs=[
            pl.BlockSpec((tm, tn), lambda i, j: (i, j)),
            pl.BlockSpec((tm, gl), lambda i, j: (i, 0)),
        ],
        out_shape=[jax.ShapeDtypeStruct((n, cols), F32), jax.ShapeDtypeStruct((n, gl), F32)],
        scratch_shapes=[pltpu.VMEM((tm, d), BF16)],
        compiler_params=pltpu.CompilerParams(
            dimension_semantics=("parallel", "arbitrary"),
            vmem_limit_bytes=_vmem_limit(block_bytes, scratch_bytes)),
        name="in_proj",
    )(x, w_all, w_gates)


LOG2E = 1.4426950408889634


def _cumsum01(tri_bf, g):
    g1 = g.astype(BF16)
    r1 = g - g1.astype(F32)
    g2 = r1.astype(BF16)
    g3 = (r1 - g2.astype(F32)).astype(BF16)
    return (jnp.dot(tri_bf, g1, preferred_element_type=F32)
            + jnp.dot(tri_bf, g2, preferred_element_type=F32)
            + jnp.dot(tri_bf, g3, preferred_element_type=F32))


def _neg_abs(x):
    bits = lax.bitcast_convert_type(x, jnp.uint32) | jnp.uint32(0x80000000)
    return lax.bitcast_convert_type(bits, F32)


def _hgrn_level_operand(q, k, f, b, m, *, reverse):
    c, d = q.shape
    sub = V7X_SUBLANES
    if m >= sub:
        refs, sides = [], []
        for p in range(c // (2 * m)):
            r = p * 2 * m + (m if reverse else m - 1)
            refs.append(jnp.broadcast_to(b[r:r + 1, :], (2 * m, d)))
            lo, hi = slice(p * 2 * m, p * 2 * m + m), slice(p * 2 * m + m, (p + 1) * 2 * m)
            sides += [q[lo], k[hi]] if reverse else [k[lo], q[hi]]
        ref = refs[0] if len(refs) == 1 else jnp.concatenate(refs, axis=0)
        y0 = jnp.concatenate(sides, axis=0)
        return (y0 * jnp.exp2(_neg_abs(b - ref))).astype(BF16)
    nb = c // sub
    s_idx = lax.broadcasted_iota(jnp.int32, (1, sub, d), 1)
    q_side = ((s_idx & m) == 0) if reverse else ((s_idx & m) != 0)
    q3, k3 = q.reshape(nb, sub, d), k.reshape(nb, sub, d)
    if m == 1:
        y3 = jnp.where(q_side, q3 * f.reshape(nb, sub, d), k3)
        return y3.reshape(c, d).astype(BF16)
    b3 = b.reshape(nb, sub, d)
    r0 = m if reverse else m - 1
    ref3 = jnp.broadcast_to(b3[:, r0:r0 + 1, :], (nb, sub, d))
    for r in range(r0 + 2 * m, sub, 2 * m):
        ref3 = jnp.where(s_idx < r - r0, ref3, jnp.broadcast_to(b3[:, r:r + 1, :], (nb, sub, d)))
    y3 = jnp.where(q_side, q3, k3) * jnp.exp2(_neg_abs(b3 - ref3))
    return y3.reshape(c, d).astype(BF16)


def _hgrn_chunk(q, k, v, f, b, st, level_map, *, reverse):
    c, d = q.shape
    bend = b[0:1, :] if reverse else b[c - 1:c, :]
    qd = (q * jnp.exp2(b)).astype(BF16)
    kd = (k * jnp.exp2(bend - b)).astype(BF16)
    vb = v.astype(BF16)
    o = _nt(qd, st.astype(BF16))
    st_new = st * jnp.exp2(bend) + _tn(vb, kd)

    attn = jnp.zeros((c, c), F32)
    m, level = 1, 0
    while m < c:
        y = _hgrn_level_operand(q, k, f, b, m, reverse=reverse)
        attn = jnp.where(level_map == level, _nt(y, y), attn)
        m, level = 2 * m, level + 1
    o = o + jnp.dot(attn.astype(BF16), vb, preferred_element_type=F32)
    o = o + jnp.sum(q * k, axis=-1, keepdims=True) * v
    return o, st_new


def _hgrn_kernel(zq_ref, zi_ref, zg_ref, zff_ref, zfb_ref, lbp_ref, ng_ref, y_ref,
                 q_ref, f_ref, b_ref, o_ref, st_ref, *, chunk):
    t, d = zq_ref.shape
    nc = t // chunk
    lbp = lbp_ref[...]

    def lower_bound(a0, a1):
        mx = jnp.maximum(a0, a1)
        e0 = jnp.exp(a0 - mx)
        e1 = jnp.exp(a1 - mx)
        return e0 / (e0 + e1)

    lbs = (lower_bound(lbp[0:1, :], lbp[1:2, :]), lower_bound(lbp[2:3, :], lbp[3:4, :]))
    zf_refs = (zff_ref, zfb_ref)
    scale = HGRN_HEAD_DIM ** -0.5

    row = lax.broadcasted_iota(jnp.int32, (chunk, chunk), 0)
    col = lax.broadcasted_iota(jnp.int32, (chunk, chunk), 1)

    def chunk_rows(cidx):
        return pl.ds(pl.multiple_of(cidx * chunk, chunk), chunk)

    tris = (jnp.where(col <= row, 1.0, 0.0).astype(BF16), jnp.where(col >= row, 1.0, 0.0).astype(BF16))

    def prepare(cidx, carry):
        rows = chunk_rows(cidx)
        zq = zq_ref[rows, :]
        q_ref[rows, :] = zq * _sigmoid(zq) * scale
        o_ref[rows, :] = jnp.zeros((chunk, d), F32)
        for direction in (0, 1):
            lb = lbs[direction]
            f = lb + (1.0 - lb) * _sigmoid(zf_refs[direction][rows, :])
            f_ref[direction, rows, :] = f
            b_ref[direction, rows, :] = _cumsum01(tris[direction], jnp.log(f) * LOG2E)
        return carry

    lax.fori_loop(0, nc, prepare, 0, unroll=4)

    differ = row ^ col
    top_bit = jnp.zeros((chunk, chunk), jnp.int32)
    m = 2
    while m < chunk:
        top_bit = top_bit + jnp.where(differ >= m, 1, 0)
        m *= 2
    level_maps = (jnp.where(row > col, top_bit, -1), jnp.where(col > row, top_bit, -1))
    st_ref[...] = jnp.zeros_like(st_ref)

    def scan(i, carry):
        for direction in (0, 1):
            rows = chunk_rows(i if direction == 0 else nc - 1 - i)
            f = f_ref[direction, rows, :]
            o, st_new = _hgrn_chunk(q_ref[rows, :], 1.0 - f, zi_ref[rows, :], f, b_ref[direction, rows, :],
                                    st_ref[direction], level_maps[direction], reverse=direction == 1)
            st_ref[direction] = st_new
            o_ref[rows, :] += o
        return carry

    lax.fori_loop(0, nc, scan, 0, unroll=2)

    ng = ng_ref[...]

    def finish(cidx, carry):
        rows = chunk_rows(cidx)
        o = o_ref[rows, :]
        o = o * lax.rsqrt(jnp.mean(o * o, axis=-1, keepdims=True) + NORM_EPS)
        zg = zg_ref[rows, :]
        y_ref[rows, :] = (o * ng * (zg * _sigmoid(zg))).astype(y_ref.dtype)
        return carry

    lax.fori_loop(0, nc, finish, 0, unroll=4)


def _hgrn(z, lbp, norm_g, *, batch, seq, heads, chunk=SCAN_CHUNK):
    d = HGRN_HEAD_DIM
    n = z.shape[0]
    assert n == batch * seq and seq % (2 * chunk) == 0

    def zspec(off):
        return pl.BlockSpec((seq, d), lambda b, h: (b, off * heads + h))

    block_bytes = 5 * seq * d * 4 + seq * d * 2
    scratch_bytes = 6 * seq * d * 4 + 2 * d * d * 4
    return pl.pallas_call(
        functools.partial(_hgrn_kernel, chunk=chunk),
        grid=(batch, heads),
        in_specs=[zspec(0), zspec(1), zspec(2), zspec(3), zspec(4),
                  pl.BlockSpec((4, d), lambda b, h: (0, h)),
                  pl.BlockSpec((1, d), lambda b, h: (0, h))],
        out_specs=pl.BlockSpec((seq, d), lambda b, h: (b, h)),
        out_shape=jax.ShapeDtypeStruct((n, heads * d), BF16),
        scratch_shapes=[pltpu.VMEM((seq, d), F32), pltpu.VMEM((2, seq, d), F32), pltpu.VMEM((2, seq, d), F32),
                        pltpu.VMEM((seq, d), F32), pltpu.VMEM((2, d, d), F32)],
        compiler_params=pltpu.CompilerParams(
            dimension_semantics=("parallel", "parallel"),
            vmem_limit_bytes=_vmem_limit(block_bytes, scratch_bytes)),
        name="hgrn",
    )(z, z, z, z, z, lbp, norm_g)


def _qk_conv_kernel(x_ref, w_ref, b_ref, o_ref, xp_ref, *, tile, q_blocks, q_scale):
    t, c = x_ref.shape
    pad = V7X_SUBLANES
    half = CONV_WIDTH // 2
    xp_ref[0:pad, :] = jnp.zeros((pad, c), F32)
    xp_ref[pad + t:2 * pad + t, :] = jnp.zeros((pad, c), F32)
    xp_ref[pad:pad + t, :] = x_ref[...]
    w = w_ref[...]
    bias = b_ref[...]
    scale = jnp.where(pl.program_id(1) < q_blocks, q_scale, 1.0)
    win = tile + 2 * pad

    def body(i, carry):
        r0 = pl.multiple_of(i * tile, tile)
        window = xp_ref[pl.ds(r0, win), :]
        acc = bias + w[half:half + 1, :] * window[pad:pad + tile, :]
        for j in range(CONV_WIDTH):
            off = j - half
            if off == 0:
                continue
            shifted = pltpu.roll(window, (win - off) % win, 0)
            acc = acc + w[j:j + 1, :] * shifted[pad:pad + tile, :]
        o_ref[pl.ds(r0, tile), :] = (acc * _sigmoid(acc) * scale).astype(o_ref.dtype)
        return carry

    lax.fori_loop(0, t // tile, body, 0)


def _qk_conv(z, conv_w, conv_b, *, batch, seq, col_block0, width, q_width, q_scale, cblk=256, tile=256):
    n = z.shape[0]
    assert seq % tile == 0 and width % cblk == 0 and q_width % cblk == 0
    block_bytes = seq * cblk * 4 + CONV_WIDTH * cblk * 4 + cblk * 4 + seq * cblk * 2
    scratch_bytes = (seq + 2 * V7X_SUBLANES) * cblk * 4
    return pl.pallas_call(
        functools.partial(_qk_conv_kernel, tile=tile, q_blocks=q_width // cblk, q_scale=q_scale),
        grid=(batch, width // cblk),
        in_specs=[pl.BlockSpec((seq, cblk), lambda b, j: (b, col_block0 + j)),
                  pl.BlockSpec((CONV_WIDTH, cblk), lambda b, j: (0, j)),
                  pl.BlockSpec((1, cblk), lambda b, j: (0, j))],
        out_specs=pl.BlockSpec((seq, cblk), lambda b, j: (b, j)),
        out_shape=jax.ShapeDtypeStruct((n, width), BF16),
        scratch_shapes=[pltpu.VMEM((seq + 2 * V7X_SUBLANES, cblk), F32)],
        compiler_params=pltpu.CompilerParams(
            dimension_semantics=("parallel", "parallel"),
            vmem_limit_bytes=_vmem_limit(block_bytes, scratch_bytes)),
        name="qk_conv",
    )(z, conv_w, conv_b)


def _mlstm_chunk(q, k, v, lf_c, ig_c, lf_r, ig_r, c_ref, n, m, *, reverse):
    c = q.shape[0]
    row = lax.broadcasted_iota(jnp.int32, (c, c), 0)
    col = lax.broadcasted_iota(jnp.int32, (c, c), 1)
    tri = (col >= row) if reverse else (col <= row)
    tri_t = (row >= col) if reverse else (row <= col)
    bc = jnp.sum(jnp.where(tri, lf_r, 0.0), axis=1, keepdims=True)
    br = jnp.sum(jnp.where(tri_t, lf_c, 0.0), axis=0, keepdims=True)
    dmat = jnp.where(tri, bc - br + ig_r, -jnp.inf)
    m_inter = bc + m
    m_t = jnp.maximum(m_inter, jnp.max(dmat, axis=1, keepdims=True))
    inter = jnp.exp(m_inter - m_t)
    sc = _nt(q, k) * jnp.exp(dmat - m_t)
    cst = c_ref[...]
    num = (jnp.dot(sc.astype(BF16), v, preferred_element_type=F32)
           + inter * jnp.dot(q, cst.astype(BF16), preferred_element_type=F32))
    qn = jnp.sum(q.astype(F32) * n, axis=1, keepdims=True)
    den = jnp.sum(sc, axis=1, keepdims=True) + inter * qn
    h = num / jnp.maximum(jnp.abs(den), jnp.exp(-m_t))

    bend = bc[0:1, :] if reverse else bc[c - 1:c, :]
    w_c = bend - bc + ig_c
    m_new = jnp.maximum(bend + m, jnp.max(w_c, axis=0, keepdims=True))
    carry_scale = jnp.exp(bend + m - m_new)
    kw = k.astype(F32) * jnp.exp(w_c - m_new)
    c_ref[...] = carry_scale * cst + _tn(kw.astype(BF16), v)
    n_new = carry_scale * n + jnp.sum(kw, axis=0, keepdims=True)
    return h, n_new, m_new


def _mlstm_kernel(q_ref, k_ref, v_ref, og_ref, gc_ref, gr_ref, bias_r_ref, bias_c_ref, ng_ref,
                  y_ref, h_ref, c_ref, *, chunk):
    t, d = q_ref.shape
    nc = t // chunk
    bias_r = bias_r_ref[...]
    bias_c = bias_c_ref[...]

    def chunk_rows(cidx):
        return pl.ds(pl.multiple_of(cidx * chunk, chunk), chunk)

    def clear(cidx, carry):
        h_ref[chunk_rows(cidx), :] = jnp.zeros((chunk, d), F32)
        return carry

    lax.fori_loop(0, nc, clear, 0, unroll=4)
    c_ref[...] = jnp.zeros_like(c_ref)

    def scan(i, carry):
        new_carry = []
        for direction in (0, 1):
            n, m = carry[direction]
            rows = chunk_rows(i if direction == 0 else nc - 1 - i)
            gc = gc_ref[rows, :] + bias_r
            gr = gr_ref[:, rows] + bias_c
            ig_c = gc[:, direction:direction + 1]
            lf_c = _log_sigmoid(gc[:, 2 + direction:3 + direction])
            ig_r = gr[direction:direction + 1, :]
            lf_r = _log_sigmoid(gr[2 + direction:3 + direction, :])
            h, n, m = _mlstm_chunk(q_ref[rows, :], k_ref[rows, :], v_ref[rows, :].astype(BF16),
                                   lf_c, ig_c, lf_r, ig_r, c_ref.at[direction], n, m,
                                   reverse=direction == 1)
            h_ref[rows, :] += h
            new_carry.append((n, m))
        return tuple(new_carry)

    init = (jnp.zeros((1, d), F32), jnp.full((1, 1), M_INIT, F32))
    lax.fori_loop(0, nc, scan, (init, init), unroll=2)

    ng = ng_ref[...]

    def finish(cidx, carry):
        rows = chunk_rows(cidx)
        h = h_ref[rows, :]
        mu = jnp.mean(h, axis=-1, keepdims=True)
        hc = h - mu
        var = jnp.mean(hc * hc, axis=-1, keepdims=True)
        y = hc * lax.rsqrt(var + NORM_EPS) * ng * _sigmoid(og_ref[rows, :])
        y_ref[rows, :] = y.astype(y_ref.dtype)
        return carry

    lax.fori_loop(0, nc, finish, 0, unroll=4)


def _mlstm(qk, z, gcol, grow, bias_r, bias_c, norm_g, *, batch, seq, heads, d, v_block0, o_block0,
           chunk=SCAN_CHUNK):
    n = z.shape[0]
    assert n == batch * seq and seq % (2 * chunk) == 0
    block_bytes = 2 * seq * d * 2 + 2 * seq * d * 4 + 2 * seq * V7X_LANES * 4 + seq * d * 2
    scratch_bytes = seq * d * 4 + 2 * d * d * 4
    return pl.pallas_call(
        functools.partial(_mlstm_kernel, chunk=chunk),
        grid=(batch, heads),
        in_specs=[
            pl.BlockSpec((seq, d), lambda b, h: (b, h)),
            pl.BlockSpec((seq, d), lambda b, h: (b, heads + h)),
            pl.BlockSpec((seq, d), lambda b, h: (b, v_block0 + h)),
            pl.BlockSpec((seq, d), lambda b, h: (b, o_block0 + h)),
            pl.BlockSpec((None, None, seq, 4), lambda b, h: (b, h, 0, 0)),
            pl.BlockSpec((None, None, 4, seq), lambda b, h: (b, h, 0, 0)),
            pl.BlockSpec((None, 1, 4), lambda b, h: (h, 0, 0)),
            pl.BlockSpec((None, 4, 1), lambda b, h: (h, 0, 0)),
            pl.BlockSpec((1, d), lambda b, h: (0, h)),
        ],
        out_specs=pl.BlockSpec((seq, d), lambda b, h: (b, h)),
        out_shape=jax.ShapeDtypeStruct((n, heads * d), BF16),
        scratch_shapes=[pltpu.VMEM((seq, d), F32), pltpu.VMEM((2, d, d), F32)],
        compiler_params=pltpu.CompilerParams(
            dimension_semantics=("parallel", "parallel"),
            vmem_limit_bytes=_vmem_limit(block_bytes, scratch_bytes)),
        name="mlstm",
    )(qk, qk, z, z, gcol, grow, bias_r, bias_c, norm_g)


def _out_proj_kernel(x_ref, yh_ref, ym_ref, wh_ref, wm_ref, g_ref, b_ref, o_ref, whbf_ref, wmbf_ref, *, alpha):
    @pl.when(pl.program_id(0) == 0)
    def _():
        whbf_ref[...] = wh_ref[...].astype(BF16)
        wmbf_ref[...] = wm_ref[...].astype(BF16)

    r = (x_ref[...] * alpha
         + jnp.dot(yh_ref[...], whbf_ref[...], preferred_element_type=F32)
         + jnp.dot(ym_ref[...], wmbf_ref[...], preferred_element_type=F32))
    mu = jnp.mean(r, axis=-1, keepdims=True)
    d = r - mu
    var = jnp.mean(d * d, axis=-1, keepdims=True)
    o_ref[...] = d * lax.rsqrt(var + LN_EPS) * g_ref[...] + b_ref[...]


def _out_proj(x, yh, ym, w, g, b, *, alpha, tm=512):
    n, d = x.shape
    kh, km = yh.shape[1], ym.shape[1]
    assert n % tm == 0 and kh == km and w.shape[0] == kh + km
    block_bytes = tm * d * 4 + tm * kh * 2 + tm * km * 2 + tm * d * 4
    weight_bytes = (kh + km) * d * 4
    return pl.pallas_call(
        functools.partial(_out_proj_kernel, alpha=alpha),
        grid=(n // tm,),
        in_specs=[
            pl.BlockSpec((tm, d), lambda i: (i, 0)),
            pl.BlockSpec((tm, kh), lambda i: (i, 0)),
            pl.BlockSpec((tm, km), lambda i: (i, 0)),
            pl.BlockSpec((kh, d), lambda i: (0, 0), pipeline_mode=pl.Buffered(1)),
            pl.BlockSpec((km, d), lambda i: (1, 0), pipeline_mode=pl.Buffered(1)),
            pl.BlockSpec((1, d), lambda i: (0, 0)),
            pl.BlockSpec((1, d), lambda i: (0, 0)),
        ],
        out_specs=pl.BlockSpec((tm, d), lambda i: (i, 0)),
        out_shape=jax.ShapeDtypeStruct((n, d), F32),
        scratch_shapes=[pltpu.VMEM((kh, d), BF16), pltpu.VMEM((km, d), BF16)],
        compiler_params=pltpu.CompilerParams(
            dimension_semantics=("arbitrary",),
            vmem_limit_bytes=_vmem_limit(block_bytes, weight_bytes + weight_bytes // 2)),
        name="out_proj",
    )(x, yh, ym, w, w, g, b)


def kernel(x, ffn1_w1, ffn1_w3, ffn1_w2, ln1_g, ln1_b, w_in, hgrn_lb, hgrn_norm_g, mlstm_conv_w,
           mlstm_conv_b, mlstm_ig_b, mlstm_fg_b, mlstm_norm_g, w_out, ln2_g, ln2_b, ffn2_w1, ffn2_w3,
           ffn2_w2, ln3_g, ln3_b):
    batch, seq, d_model = x.shape
    depth = ffn1_w1.shape[0]
    assert depth == 1 and hgrn_lb.shape[1] == 2
    alpha = (2.0 * depth) ** 0.25
    hgrn_width = hgrn_norm_g.shape[1]
    mlstm_width = mlstm_norm_g.shape[1]
    hgrn_heads = hgrn_width // HGRN_HEAD_DIM
    md = mlstm_width // MLSTM_HEADS
    main_cols = 5 * hgrn_width + 4 * mlstm_width
    n = batch * seq

    def row(p):
        return p.reshape(1, -1).astype(F32)

    xf = x.reshape(n, d_model)
    x1 = _ffn_ln(xf, ffn1_w1[0], ffn1_w3[0], ffn1_w2[0], row(ln1_g[0]), row(ln1_b[0]), alpha=alpha)

    n_gates = w_in.shape[2] - main_cols
    w_gates = jnp.pad(w_in[0, :, main_cols:], ((0, 0), (0, V7X_LANES - n_gates)))
    z, gates = _in_proj(x1, w_in[0], w_gates, cols=main_cols)

    y_h = _hgrn(z, hgrn_lb.reshape(4, hgrn_width).astype(F32), row(hgrn_norm_g[0]),
                batch=batch, seq=seq, heads=hgrn_heads)

    qk = _qk_conv(z, mlstm_conv_w[0].astype(F32), row(mlstm_conv_b[0]), batch=batch, seq=seq,
                  col_block0=5 * hgrn_width // 256, width=2 * mlstm_width, q_width=mlstm_width,
                  q_scale=md ** -0.5)

    g4 = gates[:, :n_gates].reshape(batch, seq, 4, MLSTM_HEADS)
    gcol = g4.transpose(0, 3, 1, 2)
    grow = g4.transpose(0, 3, 2, 1)
    bias4 = jnp.stack([mlstm_ig_b[0, 0], mlstm_ig_b[0, 1], mlstm_fg_b[0, 0], mlstm_fg_b[0, 1]],
                      axis=-1).astype(F32)
    y_m = _mlstm(qk, z, gcol, grow, bias4[:, None, :], bias4[:, :, None], row(mlstm_norm_g[0]),
                 batch=batch, seq=seq, heads=MLSTM_HEADS, d=md,
                 v_block0=(5 * hgrn_width + 2 * mlstm_width) // md,
                 o_block0=(5 * hgrn_width + 3 * mlstm_width) // md)

    x2 = _out_proj(x1, y_h, y_m, w_out[0], row(ln2_g[0]), row(ln2_b[0]), alpha=alpha)
    x3 = _ffn_ln(x2, ffn2_w1[0], ffn2_w3[0], ffn2_w2[0], row(ln3_g[0]), row(ln3_b[0]), alpha=alpha)
    return x3.reshape(batch, seq, d_model)
```

```python
import functools

import jax
import jax.numpy as jnp
from jax import lax
from jax.experimental import pallas as pl
from jax.experimental.pallas import tpu as pltpu

F32 = jnp.float32
BF16 = jnp.bfloat16

V7X_VMEM_BYTES = 64 * 1024 * 1024
V7X_LANES = 128
V7X_SUBLANES = 8

HGRN_HEAD_DIM = 128
MLSTM_HEADS = 4
CONV_WIDTH = 5
LN_EPS = 1e-5
NORM_EPS = 1e-6
M_INIT = -1e30
SCAN_CHUNK = 128


def _vmem_limit(block_bytes, scratch_bytes):
    need = 2 * block_bytes + scratch_bytes
    return int(min(V7X_VMEM_BYTES - (4 << 20), need + (16 << 20)))


def _nt(a, b):
    return lax.dot_general(a, b, (((1,), (1,)), ((), ())), preferred_element_type=F32)


def _tn(a, b):
    return lax.dot_general(a, b, (((0,), (0,)), ((), ())), preferred_element_type=F32)


def _sigmoid(x):
    return 1.0 / (1.0 + jnp.exp(-x))


def _log_sigmoid(x):
    return -(jnp.maximum(-x, 0.0) + jnp.log(1.0 + jnp.exp(-jnp.abs(x))))


def _ffn_ln_kernel(x_ref, w1_ref, w3_ref, w2_ref, g_ref, b_ref, o_ref, xbf_ref, *, alpha):
    j = pl.program_id(1)

    @pl.when(j == 0)
    def _():
        xbf_ref[...] = x_ref[...].astype(BF16)
        o_ref[...] = jnp.zeros_like(o_ref)

    xb = xbf_ref[...]
    h1 = jnp.dot(xb, w1_ref[...].astype(BF16), preferred_element_type=F32)
    h3 = jnp.dot(xb, w3_ref[...].astype(BF16), preferred_element_type=F32)
    gate = (h1 * _sigmoid(h1) * h3).astype(BF16)
    o_ref[...] += jnp.dot(gate, w2_ref[...].astype(BF16), preferred_element_type=F32)

    @pl.when(j == pl.num_programs(1) - 1)
    def _():
        r = x_ref[...] * alpha + 0.5 * o_ref[...]
        mu = jnp.mean(r, axis=-1, keepdims=True)
        d = r - mu
        var = jnp.mean(d * d, axis=-1, keepdims=True)
        o_ref[...] = d * lax.rsqrt(var + LN_EPS) * g_ref[...] + b_ref[...]


def _ffn_ln(x, w1, w3, w2, g, b, *, alpha, tm=1024, tf=256):
    n, d = x.shape
    f = w1.shape[1]
    assert n % tm == 0 and f % tf == 0
    block_bytes = tm * d * 4 + 3 * d * tf * 4 + tm * d * 4
    scratch_bytes = tm * d * 2 + 3 * d * tf * 2 + 3 * tm * tf * 4
    return pl.pallas_call(
        functools.partial(_ffn_ln_kernel, alpha=alpha),
        grid=(n // tm, f // tf),
        in_specs=[
            pl.BlockSpec((tm, d), lambda i, j: (i, 0)),
            pl.BlockSpec((d, tf), lambda i, j: (0, j)),
            pl.BlockSpec((d, tf), lambda i, j: (0, j)),
            pl.BlockSpec((tf, d), lambda i, j: (j, 0)),
            pl.BlockSpec((1, d), lambda i, j: (0, 0)),
            pl.BlockSpec((1, d), lambda i, j: (0, 0)),
        ],
        out_specs=pl.BlockSpec((tm, d), lambda i, j: (i, 0)),
        out_shape=jax.ShapeDtypeStruct((n, d), F32),
        scratch_shapes=[pltpu.VMEM((tm, d), BF16)],
        compiler_params=pltpu.CompilerParams(
            dimension_semantics=("parallel", "arbitrary"),
            vmem_limit_bytes=_vmem_limit(block_bytes, scratch_bytes)),
        name="ffn_ln",
    )(x, w1, w3, w2, g, b)


def _in_proj_kernel(x_ref, wt_ref, wgt_ref, z_ref, gates_ref, xbf_ref):
    @pl.when(pl.program_id(1) == 0)
    def _():
        xbf_ref[...] = x_ref[...].astype(BF16)
        gates_ref[...] = _nt(xbf_ref[...], wgt_ref[...].astype(BF16))

    z_ref[...] = _nt(xbf_ref[...], wt_ref[...].astype(BF16))


def _in_proj(x, wt, *, cols, tm=1024, tn=1024):
    n, d = x.shape
    gl = wt.shape[0] - cols
    assert n % tm == 0 and cols % tn == 0 and gl % V7X_SUBLANES == 0 and cols % gl == 0
    block_bytes = tm * d * 4 + tn * d * 4 + gl * d * 4 + tm * tn * 4 + tm * V7X_LANES * 4
    scratch_bytes = tm * d * 2 + d * tn * 2
    return pl.pallas_call(
        _in_proj_kernel,
        grid=(n // tm, cols // tn),
        in_specs=[
            pl.BlockSpec((tm, d), lambda i, j: (i, 0)),
            pl.BlockSpec((tn, d), lambda i, j: (j, 0)),
            pl.BlockSpec((gl, d), lambda i, j: (cols // gl, 0)),
        ],
        out_specs=[
            pl.BlockSpec((tm, tn), lambda i, j: (i, j)),
            pl.BlockSpec((tm, gl), lambda i, j: (i, 0)),
        ],
        out_shape=[jax.ShapeDtypeStruct((n, cols), F32), jax.ShapeDtypeStruct((n, gl), F32)],
        scratch_shapes=[pltpu.VMEM((tm, d), BF16)],
        compiler_params=pltpu.CompilerParams(
            dimension_semantics=("parallel", "arbitrary"),
            vmem_limit_bytes=_vmem_limit(block_bytes, scratch_bytes)),
        name="in_proj",
    )(x, wt, wt)


LOG2E = 1.4426950408889634


def _cumsum01(tri_bf, g):
    g1 = g.astype(BF16)
    r1 = g - g1.astype(F32)
    g2 = r1.astype(BF16)
    g3 = (r1 - g2.astype(F32)).astype(BF16)
    return (jnp.dot(tri_bf, g1, preferred_element_type=F32)
            + jnp.dot(tri_bf, g2, preferred_element_type=F32)
            + jnp.dot(tri_bf, g3, preferred_element_type=F32))


def _neg_abs(x):
    bits = lax.bitcast_convert_type(x, jnp.uint32) | jnp.uint32(0x80000000)
    return lax.bitcast_convert_type(bits, F32)


def _hgrn_level_operand(q, k, f, b, m, *, reverse):
    c, d = q.shape
    sub = V7X_SUBLANES
    if m >= sub:
        refs, sides = [], []
        for p in range(c // (2 * m)):
            r = p * 2 * m + (m if reverse else m - 1)
            refs.append(jnp.broadcast_to(b[r:r + 1, :], (2 * m, d)))
            lo, hi = slice(p * 2 * m, p * 2 * m + m), slice(p * 2 * m + m, (p + 1) * 2 * m)
            sides += [q[lo], k[hi]] if reverse else [k[lo], q[hi]]
        ref = refs[0] if len(refs) == 1 else jnp.concatenate(refs, axis=0)
        y0 = jnp.concatenate(sides, axis=0)
        return (y0 * jnp.exp2(_neg_abs(b - ref))).astype(BF16)
    nb = c // sub
    s_idx = lax.broadcasted_iota(jnp.int32, (1, sub, d), 1)
    q_side = ((s_idx & m) == 0) if reverse else ((s_idx & m) != 0)
    q3, k3 = q.reshape(nb, sub, d), k.reshape(nb, sub, d)
    if m == 1:
        y3 = jnp.where(q_side, q3 * f.reshape(nb, sub, d), k3)
        return y3.reshape(c, d).astype(BF16)
    b3 = b.reshape(nb, sub, d)
    r0 = m if reverse else m - 1
    ref3 = jnp.broadcast_to(b3[:, r0:r0 + 1, :], (nb, sub, d))
    for r in range(r0 + 2 * m, sub, 2 * m):
        ref3 = jnp.where(s_idx < r - r0, ref3, jnp.broadcast_to(b3[:, r:r + 1, :], (nb, sub, d)))
    y3 = jnp.where(q_side, q3, k3) * jnp.exp2(_neg_abs(b3 - ref3))
    return y3.reshape(c, d).astype(BF16)


def _hgrn_chunk(q, k, v, f, b, st, level_map, *, reverse):
    c, d = q.shape
    bend = b[0:1, :] if reverse else b[c - 1:c, :]
    qd = (q * jnp.exp2(b)).astype(BF16)
    kd = (k * jnp.exp2(bend - b)).astype(BF16)
    vb = v.astype(BF16)
    o = _nt(qd, st.astype(BF16))
    st_new = st * jnp.exp2(bend) + _tn(vb, kd)

    attn = jnp.zeros((c, c), F32)
    m, level = 1, 0
    while m < c:
        y = _hgrn_level_operand(q, k, f, b, m, reverse=reverse)
        attn = jnp.where(level_map == level, _nt(y, y), attn)
        m, level = 2 * m, level + 1
    o = o + jnp.dot(attn.astype(BF16), vb, preferred_element_type=F32)
    o = o + jnp.sum(q * k, axis=-1, keepdims=True) * v
    return o, st_new


def _hgrn_kernel(zq_ref, zi_ref, zg_ref, zff_ref, zfb_ref, lbp_ref, ng_ref, y_ref,
                 q_ref, f_ref, b_ref, o_ref, st_ref, *, chunk):
    t, d = zq_ref.shape
    nc = t // chunk
    lbp = lbp_ref[...]

    def lower_bound(a0, a1):
        mx = jnp.maximum(a0, a1)
        e0 = jnp.exp(a0 - mx)
        e1 = jnp.exp(a1 - mx)
        return e0 / (e0 + e1)

    lbs = (lower_bound(lbp[0:1, :], lbp[1:2, :]), lower_bound(lbp[2:3, :], lbp[3:4, :]))
    zf_refs = (zff_ref, zfb_ref)
    scale = HGRN_HEAD_DIM ** -0.5

    row = lax.broadcasted_iota(jnp.int32, (chunk, chunk), 0)
    col = lax.broadcasted_iota(jnp.int32, (chunk, chunk), 1)

    def chunk_rows(cidx):
        return pl.ds(pl.multiple_of(cidx * chunk, chunk), chunk)

    tris = (jnp.where(col <= row, 1.0, 0.0).astype(BF16), jnp.where(col >= row, 1.0, 0.0).astype(BF16))

    def prepare(cidx, carry):
        rows = chunk_rows(cidx)
        zq = zq_ref[rows, :]
        q_ref[rows, :] = zq * _sigmoid(zq) * scale
        o_ref[rows, :] = jnp.zeros((chunk, d), F32)
        for direction in (0, 1):
            lb = lbs[direction]
            f = lb + (1.0 - lb) * _sigmoid(zf_refs[direction][rows, :])
            f_ref[direction, rows, :] = f
            b_ref[direction, rows, :] = _cumsum01(tris[direction], jnp.log(f) * LOG2E)
        return carry

    lax.fori_loop(0, nc, prepare, 0, unroll=4)

    differ = row ^ col
    top_bit = jnp.zeros((chunk, chunk), jnp.int32)
    m = 2
    while m < chunk:
        top_bit = top_bit + jnp.where(differ >= m, 1, 0)
        m *= 2
    level_maps = (jnp.where(row > col, top_bit, -1), jnp.where(col > row, top_bit, -1))
    st_ref[...] = jnp.zeros_like(st_ref)

    def scan(i, carry):
        for direction in (0, 1):
            rows = chunk_rows(i if direction == 0 else nc - 1 - i)
            f = f_ref[direction, rows, :]
            o, st_new = _hgrn_chunk(q_ref[rows, :], 1.0 - f, zi_ref[rows, :], f, b_ref[direction, rows, :],
                                    st_ref[direction], level_maps[direction], reverse=direction == 1)
            st_ref[direction] = st_new
            o_ref[rows, :] += o
        return carry

    lax.fori_loop(0, nc, scan, 0, unroll=4)

    ng = ng_ref[...]

    def finish(cidx, carry):
        rows = chunk_rows(cidx)
        o = o_ref[rows, :]
        o = o * lax.rsqrt(jnp.mean(o * o, axis=-1, keepdims=True) + NORM_EPS)
        zg = zg_ref[rows, :]
        y_ref[rows, :] = (o * ng * (zg * _sigmoid(zg))).astype(y_ref.dtype)
        return carry

    lax.fori_loop(0, nc, finish, 0, unroll=4)


def _hgrn(z, lbp, norm_g, *, batch, seq, heads, chunk=SCAN_CHUNK):
    d = HGRN_HEAD_DIM
    n = z.shape[0]
    assert n == batch * seq and seq % (2 * chunk) == 0

    def zspec(off):
        return pl.BlockSpec((seq, d), lambda b, h: (b, off * heads + h))

    block_bytes = 5 * seq * d * 4 + seq * d * 2
    scratch_bytes = 6 * seq * d * 4 + 2 * d * d * 4
    return pl.pallas_call(
        functools.partial(_hgrn_kernel, chunk=chunk),
        grid=(batch, heads),
        in_specs=[zspec(0), zspec(1), zspec(2), zspec(3), zspec(4),
                  pl.BlockSpec((4, d), lambda b, h: (0, h)),
                  pl.BlockSpec((1, d), lambda b, h: (0, h))],
        out_specs=pl.BlockSpec((seq, d), lambda b, h: (b, h)),
        out_shape=jax.ShapeDtypeStruct((n, heads * d), BF16),
        scratch_shapes=[pltpu.VMEM((seq, d), F32), pltpu.VMEM((2, seq, d), F32), pltpu.VMEM((2, seq, d), F32),
                        pltpu.VMEM((seq, d), F32), pltpu.VMEM((2, d, d), F32)],
        compiler_params=pltpu.CompilerParams(
            dimension_semantics=("parallel", "parallel"),
            vmem_limit_bytes=_vmem_limit(block_bytes, scratch_bytes)),
        name="hgrn",
    )(z, z, z, z, z, lbp, norm_g)


def _qk_conv_kernel(x_ref, w_ref, b_ref, o_ref, xp_ref, *, tile, q_blocks, q_scale):
    t, c = x_ref.shape
    pad = V7X_SUBLANES
    half = CONV_WIDTH // 2
    xp_ref[0:pad, :] = jnp.zeros((pad, c), F32)
    xp_ref[pad + t:2 * pad + t, :] = jnp.zeros((pad, c), F32)
    xp_ref[pad:pad + t, :] = x_ref[...]
    w = w_ref[...]
    bias = b_ref[...]
    scale = jnp.where(pl.program_id(1) < q_blocks, q_scale, 1.0)
    win = tile + 2 * pad

    def body(i, carry):
        r0 = pl.multiple_of(i * tile, tile)
        window = xp_ref[pl.ds(r0, win), :]
        acc = bias + w[half:half + 1, :] * window[pad:pad + tile, :]
        for j in range(CONV_WIDTH):
            off = j - half
            if off == 0:
                continue
            shifted = pltpu.roll(window, (win - off) % win, 0)
            acc = acc + w[j:j + 1, :] * shifted[pad:pad + tile, :]
        o_ref[pl.ds(r0, tile), :] = (acc * _sigmoid(acc) * scale).astype(o_ref.dtype)
        return carry

    lax.fori_loop(0, t // tile, body, 0)


def _qk_conv(z, conv_w, conv_b, *, batch, seq, col_block0, width, q_width, q_scale, cblk=256, tile=256):
    n = z.shape[0]
    assert seq % tile == 0 and width % cblk == 0 and q_width % cblk == 0
    block_bytes = seq * cblk * 4 + CONV_WIDTH * cblk * 4 + cblk * 4 + seq * cblk * 2
    scratch_bytes = (seq + 2 * V7X_SUBLANES) * cblk * 4
    return pl.pallas_call(
        functools.partial(_qk_conv_kernel, tile=tile, q_blocks=q_width // cblk, q_scale=q_scale),
        grid=(batch, width // cblk),
        in_specs=[pl.BlockSpec((seq, cblk), lambda b, j: (b, col_block0 + j)),
                  pl.BlockSpec((CONV_WIDTH, cblk), lambda b, j: (0, j)),
                  pl.BlockSpec((1, cblk), lambda b, j: (0, j))],
        out_specs=pl.BlockSpec((seq, cblk), lambda b, j: (b, j)),
        out_shape=jax.ShapeDtypeStruct((n, width), BF16),
        scratch_shapes=[pltpu.VMEM((seq + 2 * V7X_SUBLANES, cblk), F32)],
        compiler_params=pltpu.CompilerParams(
            dimension_semantics=("parallel", "parallel"),
            vmem_limit_bytes=_vmem_limit(block_bytes, scratch_bytes)),
        name="qk_conv",
    )(z, conv_w, conv_b)


def _mlstm_chunk(q, k, v, lf_c, ig_c, lf_r, ig_r, c_ref, n, m, *, reverse):
    c = q.shape[0]
    row = lax.broadcasted_iota(jnp.int32, (c, c), 0)
    col = lax.broadcasted_iota(jnp.int32, (c, c), 1)
    tri = (col >= row) if reverse else (col <= row)
    tri_t = (row >= col) if reverse else (row <= col)
    bc = jnp.sum(jnp.where(tri, lf_r, 0.0), axis=1, keepdims=True)
    br = jnp.sum(jnp.where(tri_t, lf_c, 0.0), axis=0, keepdims=True)
    dmat = jnp.where(tri, bc - br + ig_r, -jnp.inf)
    m_inter = bc + m
    m_t = jnp.maximum(m_inter, jnp.max(dmat, axis=1, keepdims=True))
    inter = jnp.exp(m_inter - m_t)
    sc = _nt(q, k) * jnp.exp(dmat - m_t)
    cst = c_ref[...]
    num = (jnp.dot(sc.astype(BF16), v, preferred_element_type=F32)
           + inter * jnp.dot(q, cst.astype(BF16), preferred_element_type=F32))
    qn = jnp.sum(q.astype(F32) * n, axis=1, keepdims=True)
    den = jnp.sum(sc, axis=1, keepdims=True) + inter * qn
    h = num / jnp.maximum(jnp.abs(den), jnp.exp(-m_t))

    bend = bc[0:1, :] if reverse else bc[c - 1:c, :]
    w_c = bend - bc + ig_c
    m_new = jnp.maximum(bend + m, jnp.max(w_c, axis=0, keepdims=True))
    carry_scale = jnp.exp(bend + m - m_new)
    kw = k.astype(F32) * jnp.exp(w_c - m_new)
    c_ref[...] = carry_scale * cst + _tn(kw.astype(BF16), v)
    n_new = carry_scale * n + jnp.sum(kw, axis=0, keepdims=True)
    return h, n_new, m_new


def _mlstm_kernel(q_ref, k_ref, v_ref, og_ref, gc_ref, gr_ref, bias_r_ref, bias_c_ref, ng_ref,
                  y_ref, h_ref, c_ref, *, chunk):
    t, d = q_ref.shape
    nc = t // chunk
    bias_r = bias_r_ref[...]
    bias_c = bias_c_ref[...]

    def chunk_rows(cidx):
        return pl.ds(pl.multiple_of(cidx * chunk, chunk), chunk)

    def clear(cidx, carry):
        h_ref[chunk_rows(cidx), :] = jnp.zeros((chunk, d), F32)
        return carry

    lax.fori_loop(0, nc, clear, 0, unroll=4)
    c_ref[...] = jnp.zeros_like(c_ref)

    def scan(i, carry):
        new_carry = []
        for direction in (0, 1):
            n, m = carry[direction]
            rows = chunk_rows(i if direction == 0 else nc - 1 - i)
            gc = gc_ref[rows, :] + bias_r
            gr = gr_ref[:, rows] + bias_c
            ig_c = gc[:, direction:direction + 1]
            lf_c = _log_sigmoid(gc[:, 2 + direction:3 + direction])
            ig_r = gr[direction:direction + 1, :]
            lf_r = _log_sigmoid(gr[2 + direction:3 + direction, :])
            h, n, m = _mlstm_chunk(q_ref[rows, :], k_ref[rows, :], v_ref[rows, :].astype(BF16),
                                   lf_c, ig_c, lf_r, ig_r, c_ref.at[direction], n, m,
                                   reverse=direction == 1)
            h_ref[rows, :] += h
            new_carry.append((n, m))
        return tuple(new_carry)

    init = (jnp.zeros((1, d), F32), jnp.full((1, 1), M_INIT, F32))
    lax.fori_loop(0, nc, scan, (init, init), unroll=2)

    ng = ng_ref[...]

    def finish(cidx, carry):
        rows = chunk_rows(cidx)
        h = h_ref[rows, :]
        mu = jnp.mean(h, axis=-1, keepdims=True)
        hc = h - mu
        var = jnp.mean(hc * hc, axis=-1, keepdims=True)
        y = hc * lax.rsqrt(var + NORM_EPS) * ng * _sigmoid(og_ref[rows, :])
        y_ref[rows, :] = y.astype(y_ref.dtype)
        return carry

    lax.fori_loop(0, nc, finish, 0, unroll=4)


def _mlstm(qk, z, gcol, grow, bias_r, bias_c, norm_g, *, batch, seq, heads, d, v_block0, o_block0,
           chunk=SCAN_CHUNK):
    n = z.shape[0]
    assert n == batch * seq and seq % (2 * chunk) == 0
    block_bytes = 2 * seq * d * 2 + 2 * seq * d * 4 + 2 * seq * V7X_LANES * 4 + seq * d * 2
    scratch_bytes = seq * d * 4 + 2 * d * d * 4
    return pl.pallas_call(
        functools.partial(_mlstm_kernel, chunk=chunk),
        grid=(batch, heads),
        in_specs=[
            pl.BlockSpec((seq, d), lambda b, h: (b, h)),
            pl.BlockSpec((seq, d), lambda b, h: (b, heads + h)),
            pl.BlockSpec((seq, d), lambda b, h: (b, v_block0 + h)),
            pl.BlockSpec((seq, d), lambda b, h: (b, o_block0 + h)),
            pl.BlockSpec((None, None, seq, 4), lambda b, h: (b, h, 0, 0)),
            pl.BlockSpec((None, None, 4, seq), lambda b, h: (b, h, 0, 0)),
            pl.BlockSpec((None, 1, 4), lambda b, h: (h, 0, 0)),
            pl.BlockSpec((None, 4, 1), lambda b, h: (h, 0, 0)),
            pl.BlockSpec((1, d), lambda b, h: (0, h)),
        ],
        out_specs=pl.BlockSpec((seq, d), lambda b, h: (b, h)),
        out_shape=jax.ShapeDtypeStruct((n, heads * d), BF16),
        scratch_shapes=[pltpu.VMEM((seq, d), F32), pltpu.VMEM((2, d, d), F32)],
        compiler_params=pltpu.CompilerParams(
            dimension_semantics=("parallel", "parallel"),
            vmem_limit_bytes=_vmem_limit(block_bytes, scratch_bytes)),
        name="mlstm",
    )(qk, qk, z, z, gcol, grow, bias_r, bias_c, norm_g)


def _out_proj_kernel(x_ref, yh_ref, ym_ref, wh_ref, wm_ref, g_ref, b_ref, o_ref, whbf_ref, wmbf_ref, *, alpha):
    @pl.when(pl.program_id(0) == 0)
    def _():
        whbf_ref[...] = wh_ref[...].astype(BF16)
        wmbf_ref[...] = wm_ref[...].astype(BF16)

    r = (x_ref[...] * alpha
         + jnp.dot(yh_ref[...], whbf_ref[...], preferred_element_type=F32)
         + jnp.dot(ym_ref[...], wmbf_ref[...], preferred_element_type=F32))
    mu = jnp.mean(r, axis=-1, keepdims=True)
    d = r - mu
    var = jnp.mean(d * d, axis=-1, keepdims=True)
    o_ref[...] = d * lax.rsqrt(var + LN_EPS) * g_ref[...] + b_ref[...]


def _out_proj(x, yh, ym, w, g, b, *, alpha, tm=512):
    n, d = x.shape
    kh, km = yh.shape[1], ym.shape[1]
    assert n % tm == 0 and kh == km and w.shape[0] == kh + km
    block_bytes = tm * d * 4 + tm * kh * 2 + tm * km * 2 + tm * d * 4
    weight_bytes = (kh + km) * d * 4
    return pl.pallas_call(
        functools.partial(_out_proj_kernel, alpha=alpha),
        grid=(n // tm,),
        in_specs=[
            pl.BlockSpec((tm, d), lambda i: (i, 0)),
            pl.BlockSpec((tm, kh), lambda i: (i, 0)),
            pl.BlockSpec((tm, km), lambda i: (i, 0)),
            pl.BlockSpec((kh, d), lambda i: (0, 0), pipeline_mode=pl.Buffered(1)),
            pl.BlockSpec((km, d), lambda i: (1, 0), pipeline_mode=pl.Buffered(1)),
            pl.BlockSpec((1, d), lambda i: (0, 0)),
            pl.BlockSpec((1, d), lambda i: (0, 0)),
        ],
        out_specs=pl.BlockSpec((tm, d), lambda i: (i, 0)),
        out_shape=jax.ShapeDtypeStruct((n, d), F32),
        scratch_shapes=[pltpu.VMEM((kh, d), BF16), pltpu.VMEM((km, d), BF16)],
        compiler_params=pltpu.CompilerParams(
            dimension_semantics=("arbitrary",),
            vmem_limit_bytes=_vmem_limit(block_bytes, weight_bytes + weight_bytes // 2)),
        name="out_proj",
    )(x, yh, ym, w, w, g, b)


def kernel(x, ffn1_w1, ffn1_w3, ffn1_w2, ln1_g, ln1_b, w_in, hgrn_lb, hgrn_norm_g, mlstm_conv_w,
           mlstm_conv_b, mlstm_ig_b, mlstm_fg_b, mlstm_norm_g, w_out, ln2_g, ln2_b, ffn2_w1, ffn2_w3,
           ffn2_w2, ln3_g, ln3_b):
    batch, seq, d_model = x.shape
    depth = ffn1_w1.shape[0]
    assert depth == 1 and hgrn_lb.shape[1] == 2
    alpha = (2.0 * depth) ** 0.25
    hgrn_width = hgrn_norm_g.shape[1]
    mlstm_width = mlstm_norm_g.shape[1]
    hgrn_heads = hgrn_width // HGRN_HEAD_DIM
    md = mlstm_width // MLSTM_HEADS
    main_cols = 5 * hgrn_width + 4 * mlstm_width
    n = batch * seq

    def row(p):
        return p.reshape(1, -1).astype(F32)

    xf = x.reshape(n, d_model)
    x1 = _ffn_ln(xf, ffn1_w1[0], ffn1_w3[0], ffn1_w2[0], row(ln1_g[0]), row(ln1_b[0]), alpha=alpha)

    n_gates = w_in.shape[2] - main_cols
    z, gates = _in_proj(x1, w_in[0].T, cols=main_cols)

    y_h = _hgrn(z, hgrn_lb.reshape(4, hgrn_width).astype(F32), row(hgrn_norm_g[0]),
                batch=batch, seq=seq, heads=hgrn_heads)

    qk = _qk_conv(z, mlstm_conv_w[0].astype(F32), row(mlstm_conv_b[0]), batch=batch, seq=seq,
                  col_block0=5 * hgrn_width // 256, width=2 * mlstm_width, q_width=mlstm_width,
                  q_scale=md ** -0.5)

    g4 = gates[:, :n_gates].reshape(batch, seq, 4, MLSTM_HEADS)
    gcol = g4.transpose(0, 3, 1, 2)
    grow = g4.transpose(0, 3, 2, 1)
    bias4 = jnp.stack([mlstm_ig_b[0, 0], mlstm_ig_b[0, 1], mlstm_fg_b[0, 0], mlstm_fg_b[0, 1]],
                      axis=-1).astype(F32)
    y_m = _mlstm(qk, z, gcol, grow, bias4[:, None, :], bias4[:, :, None], row(mlstm_norm_g[0]),
                 batch=batch, seq=seq, heads=MLSTM_HEADS, d=md,
                 v_block0=(5 * hgrn_width + 2 * mlstm_width) // md,
                 o_block0=(5 * hgrn_width + 3 * mlstm_width) // md)

    x2 = _out_proj(x1, y_h, y_m, w_out[0], row(ln2_g[0]), row(ln2_b[0]), alpha=alpha)
    x3 = _ffn_ln(x2, ffn2_w1[0], ffn2_w3[0], ffn2_w2[0], row(ln3_g[0]), row(ln3_b[0]), alpha=alpha)
    return x3.reshape(batch, seq, d_model)
```

```python
import functools

import jax
import jax.numpy as jnp
from jax import lax
from jax.experimental import pallas as pl
from jax.experimental.pallas import tpu as pltpu

F32 = jnp.float32
BF16 = jnp.bfloat16

V7X_VMEM_BYTES = 64 * 1024 * 1024
V7X_LANES = 128
V7X_SUBLANES = 8

HGRN_HEAD_DIM = 128
MLSTM_HEADS = 4
CONV_WIDTH = 5
LN_EPS = 1e-5
NORM_EPS = 1e-6
M_INIT = -1e30
SCAN_CHUNK = 128


def _vmem_limit(block_bytes, scratch_bytes):
    need = 2 * block_bytes + scratch_bytes
    return int(min(V7X_VMEM_BYTES - (4 << 20), need + (16 << 20)))


def _nt(a, b):
    return lax.dot_general(a, b, (((1,), (1,)), ((), ())), preferred_element_type=F32)


def _tn(a, b):
    return lax.dot_general(a, b, (((0,), (0,)), ((), ())), preferred_element_type=F32)


def _sigmoid(x):
    return 1.0 / (1.0 + jnp.exp(-x))


def _log_sigmoid(x):
    return -(jnp.maximum(-x, 0.0) + jnp.log(1.0 + jnp.exp(-jnp.abs(x))))


def _ffn_ln_kernel(x_ref, w1_ref, w3_ref, w2_ref, g_ref, b_ref, o_ref, xbf_ref, *, alpha):
    j = pl.program_id(1)

    @pl.when(j == 0)
    def _():
        xbf_ref[...] = x_ref[...].astype(BF16)
        o_ref[...] = jnp.zeros_like(o_ref)

    xb = xbf_ref[...]
    h1 = jnp.dot(xb, w1_ref[...].astype(BF16), preferred_element_type=F32)
    h3 = jnp.dot(xb, w3_ref[...].astype(BF16), preferred_element_type=F32)
    gate = (h1 * _sigmoid(h1) * h3).astype(BF16)
    o_ref[...] += jnp.dot(gate, w2_ref[...].astype(BF16), preferred_element_type=F32)

    @pl.when(j == pl.num_programs(1) - 1)
    def _():
        r = x_ref[...] * alpha + 0.5 * o_ref[...]
        mu = jnp.mean(r, axis=-1, keepdims=True)
        d = r - mu
        var = jnp.mean(d * d, axis=-1, keepdims=True)
        o_ref[...] = d * lax.rsqrt(var + LN_EPS) * g_ref[...] + b_ref[...]


def _ffn_ln(x, w1, w3, w2, g, b, *, alpha, tm=1024, tf=256):
    n, d = x.shape
    f = w1.shape[1]
    assert n % tm == 0 and f % tf == 0
    block_bytes = tm * d * 4 + 3 * d * tf * 4 + tm * d * 4
    scratch_bytes = tm * d * 2 + 3 * d * tf * 2 + 3 * tm * tf * 4
    return pl.pallas_call(
        functools.partial(_ffn_ln_kernel, alpha=alpha),
        grid=(n // tm, f // tf),
        in_specs=[
            pl.BlockSpec((tm, d), lambda i, j: (i, 0)),
            pl.BlockSpec((d, tf), lambda i, j: (0, j)),
            pl.BlockSpec((d, tf), lambda i, j: (0, j)),
            pl.BlockSpec((tf, d), lambda i, j: (j, 0)),
            pl.BlockSpec((1, d), lambda i, j: (0, 0)),
            pl.BlockSpec((1, d), lambda i, j: (0, 0)),
        ],
        out_specs=pl.BlockSpec((tm, d), lambda i, j: (i, 0)),
        out_shape=jax.ShapeDtypeStruct((n, d), F32),
        scratch_shapes=[pltpu.VMEM((tm, d), BF16)],
        compiler_params=pltpu.CompilerParams(
            dimension_semantics=("parallel", "arbitrary"),
            vmem_limit_bytes=_vmem_limit(block_bytes, scratch_bytes)),
        name="ffn_ln",
    )(x, w1, w3, w2, g, b)


def _in_proj_head_kernel(x_ref, wt_ref, wgt_ref, z_ref, gates_ref, xbf_ref, wbf_ref, wgbf_ref):
    @pl.when(pl.program_id(0) == 0)
    def _():
        wbf_ref[...] = wt_ref[...].astype(BF16)
        wgbf_ref[...] = wgt_ref[...].astype(BF16)

    xb = x_ref[...].astype(BF16)
    xbf_ref[...] = xb
    z_ref[...] = _nt(xb, wbf_ref[...])
    gates_ref[...] = _nt(xb, wgbf_ref[...])


def _in_proj_tail_kernel(xbf_ref, wt_ref, z_ref, wbf_ref):
    @pl.when(pl.program_id(1) == 0)
    def _():
        wbf_ref[...] = wt_ref[...].astype(BF16)

    z_ref[...] = _nt(xbf_ref[...], wbf_ref[...])


def _in_proj(x, wt, *, cols, tm=1024, tn=1024):
    n, d = x.shape
    gl = wt.shape[0] - cols
    assert n % tm == 0 and cols % tn == 0 and cols > tn and gl % V7X_SUBLANES == 0 and cols % gl == 0
    head_blocks = tm * d * 4 + tm * tn * 4 + tm * V7X_LANES * 4 + tm * d * 2
    head_weights = tn * d * 4 + gl * d * 4
    z_head, gates, xbf = pl.pallas_call(
        _in_proj_head_kernel,
        grid=(n // tm,),
        in_specs=[
            pl.BlockSpec((tm, d), lambda i: (i, 0)),
            pl.BlockSpec((tn, d), lambda i: (0, 0), pipeline_mode=pl.Buffered(1)),
            pl.BlockSpec((gl, d), lambda i: (cols // gl, 0), pipeline_mode=pl.Buffered(1)),
        ],
        out_specs=[
            pl.BlockSpec((tm, tn), lambda i: (i, 0)),
            pl.BlockSpec((tm, gl), lambda i: (i, 0)),
            pl.BlockSpec((tm, d), lambda i: (i, 0)),
        ],
        out_shape=[jax.ShapeDtypeStruct((n, tn), F32), jax.ShapeDtypeStruct((n, gl), F32),
                   jax.ShapeDtypeStruct((n, d), BF16)],
        scratch_shapes=[pltpu.VMEM((tn, d), BF16), pltpu.VMEM((gl, d), BF16)],
        compiler_params=pltpu.CompilerParams(
            dimension_semantics=("arbitrary",),
            vmem_limit_bytes=_vmem_limit(head_blocks, head_weights + head_weights // 2)),
        name="in_proj_head",
    )(x, wt, wt)

    tail_blocks = tm * d * 2 + tn * d * 4 + tm * tn * 4
    z_tail = pl.pallas_call(
        _in_proj_tail_kernel,
        grid=(cols // tn - 1, n // tm),
        in_specs=[
            pl.BlockSpec((tm, d), lambda j, i: (i, 0)),
            pl.BlockSpec((tn, d), lambda j, i: (j + 1, 0)),
        ],
        out_specs=pl.BlockSpec((tm, tn), lambda j, i: (i, j)),
        out_shape=jax.ShapeDtypeStruct((n, cols - tn), F32),
        scratch_shapes=[pltpu.VMEM((tn, d), BF16)],
        compiler_params=pltpu.CompilerParams(
            dimension_semantics=("parallel", "arbitrary"),
            vmem_limit_bytes=_vmem_limit(tail_blocks, tn * d * 2)),
        name="in_proj_tail",
    )(xbf, wt)
    return z_head, z_tail, gates


LOG2E = 1.4426950408889634
HGRN_FLAT_BLOCK = 128
HGRN_FLAT_LOG2_RANGE = 100.0


def _cumsum01(tri_bf, g):
    g1 = g.astype(BF16)
    r1 = g - g1.astype(F32)
    g2 = r1.astype(BF16)
    g3 = (r1 - g2.astype(F32)).astype(BF16)
    return (jnp.dot(tri_bf, g1, preferred_element_type=F32)
            + jnp.dot(tri_bf, g2, preferred_element_type=F32)
            + jnp.dot(tri_bf, g3, preferred_element_type=F32))


def _neg_abs(x):
    bits = lax.bitcast_convert_type(x, jnp.uint32) | jnp.uint32(0x80000000)
    return lax.bitcast_convert_type(bits, F32)


def _hgrn_level_operand(q, k, f, b, m, *, reverse):
    c, d = q.shape
    sub = V7X_SUBLANES
    if m >= sub:
        refs, sides = [], []
        for p in range(c // (2 * m)):
            r = p * 2 * m + (m if reverse else m - 1)
            refs.append(jnp.broadcast_to(b[r:r + 1, :], (2 * m, d)))
            lo, hi = slice(p * 2 * m, p * 2 * m + m), slice(p * 2 * m + m, (p + 1) * 2 * m)
            sides += [q[lo], k[hi]] if reverse else [k[lo], q[hi]]
        ref = refs[0] if len(refs) == 1 else jnp.concatenate(refs, axis=0)
        y0 = jnp.concatenate(sides, axis=0)
        return y0 * jnp.exp2(_neg_abs(b - ref))
    nb = c // sub
    s_idx = lax.broadcasted_iota(jnp.int32, (1, sub, d), 1)
    q_side = ((s_idx & m) == 0) if reverse else ((s_idx & m) != 0)
    q3, k3 = q.reshape(nb, sub, d), k.reshape(nb, sub, d)
    if m == 1:
        y3 = jnp.where(q_side, q3 * f.reshape(nb, sub, d), k3)
        return y3.reshape(c, d)
    b3 = b.reshape(nb, sub, d)
    r0 = m if reverse else m - 1
    ref3 = jnp.broadcast_to(b3[:, r0:r0 + 1, :], (nb, sub, d))
    for r in range(r0 + 2 * m, sub, 2 * m):
        ref3 = jnp.where(s_idx < r - r0, ref3, jnp.broadcast_to(b3[:, r:r + 1, :], (nb, sub, d)))
    y3 = jnp.where(q_side, q3, k3) * jnp.exp2(_neg_abs(b3 - ref3))
    return y3.reshape(c, d)


def _hgrn_flat_operands(q, k, b, block, *, reverse):
    c, d = q.shape
    refs = []
    for p in range(c // block):
        r = p * block + (block - 1 if reverse else 0)
        refs.append(jnp.broadcast_to(b[r:r + 1, :], (block, d)))
    e = b - jnp.concatenate(refs, axis=0)
    return q * jnp.exp2(e), k * jnp.exp2(-e)


def _hgrn_chunk(q, k, v, f, b, st, level_map, flat_mask, *, reverse, flat_block):
    c, d = q.shape
    bend = b[0:1, :] if reverse else b[c - 1:c, :]
    qd = (q * jnp.exp2(b)).astype(BF16)
    kd = (k * jnp.exp2(bend - b)).astype(BF16)
    vb = v.astype(BF16)
    if flat_block:
        o = _nt(qd, st.astype(BF16))
        st_new = st * jnp.exp2(bend) + _tn(vb, kd)
        yq, yk = _hgrn_flat_operands(q, k, b, flat_block, reverse=reverse)
        scores = jnp.dot(yq.astype(BF16), yk.T.astype(BF16), preferred_element_type=F32)
        attn = jnp.where(flat_mask, scores, 0.0)
        m, level = flat_block, flat_block.bit_length() - 1
    else:
        o = jnp.dot(qd, st.astype(BF16), preferred_element_type=F32)
        decay_col = jnp.broadcast_to(jnp.exp2(bend), (d, d)).T
        st_new = st * decay_col + _tn(kd, vb)
        attn = jnp.zeros((c, c), F32)
        m, level = 1, 0
    while m < c:
        y = _hgrn_level_operand(q, k, f, b, m, reverse=reverse)
        scores = jnp.dot(y.astype(BF16), y.T.astype(BF16), preferred_element_type=F32)
        attn = jnp.where(level_map == level, scores, attn)
        m, level = 2 * m, level + 1
    o = o + jnp.dot(attn.astype(BF16), vb, preferred_element_type=F32)
    if not flat_block:
        o = o + jnp.sum(q * k, axis=-1, keepdims=True) * v
    return o, st_new


def _hgrn_kernel(zq_ref, zi_ref, zg_ref, zff_ref, zfb_ref, lbp_ref, ng_ref, y_ref,
                 q_ref, f_ref, b_ref, o_ref, *, chunk):
    t, d = zq_ref.shape
    nc = t // chunk
    lbp = lbp_ref[...]

    def lower_bound(a0, a1):
        mx = jnp.maximum(a0, a1)
        e0 = jnp.exp(a0 - mx)
        e1 = jnp.exp(a1 - mx)
        return e0 / (e0 + e1)

    lbs = (lower_bound(lbp[0:1, :], lbp[1:2, :]), lower_bound(lbp[2:3, :], lbp[3:4, :]))
    zf_refs = (zff_ref, zfb_ref)
    scale = HGRN_HEAD_DIM ** -0.5

    row = lax.broadcasted_iota(jnp.int32, (chunk, chunk), 0)
    col = lax.broadcasted_iota(jnp.int32, (chunk, chunk), 1)

    def chunk_rows(cidx):
        return pl.ds(pl.multiple_of(cidx * chunk, chunk), chunk)

    tris = (jnp.where(col <= row, 1.0, 0.0).astype(BF16), jnp.where(col >= row, 1.0, 0.0).astype(BF16))

    def prepare(cidx, spread):
        rows = chunk_rows(cidx)
        zq = zq_ref[rows, :]
        q_ref[rows, :] = zq * _sigmoid(zq) * scale
        for direction in (0, 1):
            lb = lbs[direction]
            f = lb + (1.0 - lb) * _sigmoid(zf_refs[direction][rows, :])
            f_ref[direction, rows, :] = f
            b = _cumsum01(tris[direction], jnp.log(f) * LOG2E)
            b_ref[direction, rows, :] = b
            for r0 in range(0, chunk, HGRN_FLAT_BLOCK):
                spread = jnp.maximum(spread, jnp.abs(b[r0:r0 + 1, :] - b[r0 + HGRN_FLAT_BLOCK - 1:r0 + HGRN_FLAT_BLOCK, :]))
        return spread

    spread = lax.fori_loop(0, nc, prepare, jnp.zeros((1, d), F32), unroll=4)
    bounded = jnp.max(spread) <= HGRN_FLAT_LOG2_RANGE

    differ = row ^ col
    top_bit = jnp.zeros((chunk, chunk), jnp.int32)
    m = 2
    while m < chunk:
        top_bit = top_bit + jnp.where(differ >= m, 1, 0)
        m *= 2
    level_maps = (jnp.where(row > col, top_bit, -1), jnp.where(col > row, top_bit, -1))
    same_block = (row // HGRN_FLAT_BLOCK) == (col // HGRN_FLAT_BLOCK)
    flat_masks = (same_block & (row >= col), same_block & (row <= col))

    def run_scan(flat_block):
        def scan(i, states):
            new_states = []
            for direction in (0, 1):
                rows = chunk_rows(i if direction == 0 else nc - 1 - i)
                f = f_ref[direction, rows, :]
                o, st_new = _hgrn_chunk(q_ref[rows, :], 1.0 - f, zi_ref[rows, :], f, b_ref[direction, rows, :],
                                        states[direction], level_maps[direction], flat_masks[direction],
                                        reverse=direction == 1, flat_block=flat_block)
                new_states.append(st_new)
                o_ref[direction, rows, :] = o
            return tuple(new_states)

        lax.fori_loop(0, nc, scan, (jnp.zeros((d, d), F32), jnp.zeros((d, d), F32)),
                      unroll=8 if flat_block else 4)
        return 0

    lax.cond(bounded, lambda: run_scan(HGRN_FLAT_BLOCK), lambda: run_scan(0))

    ng = ng_ref[...]

    def finish(cidx, carry):
        rows = chunk_rows(cidx)
        o = o_ref[0, rows, :] + o_ref[1, rows, :]
        o = o * lax.rsqrt(jnp.mean(o * o, axis=-1, keepdims=True) + NORM_EPS)
        zg = zg_ref[rows, :]
        y_ref[rows, :] = (o * ng * (zg * _sigmoid(zg))).astype(y_ref.dtype)
        return carry

    lax.fori_loop(0, nc, finish, 0, unroll=4)


def _hgrn(zq, z, lbp, norm_g, *, batch, seq, heads, chunk=SCAN_CHUNK):
    d = HGRN_HEAD_DIM
    n = zq.shape[0]
    assert n == batch * seq and seq % (2 * chunk) == 0
    assert zq.shape[1] == heads * d

    def zspec(off):
        return pl.BlockSpec((seq, d), lambda b, h: (b, off * heads + h))

    block_bytes = 5 * seq * d * 4 + seq * d * 2
    scratch_bytes = 7 * seq * d * 4
    return pl.pallas_call(
        functools.partial(_hgrn_kernel, chunk=chunk),
        grid=(batch, heads),
        in_specs=[zspec(0), zspec(0), zspec(1), zspec(2), zspec(3),
                  pl.BlockSpec((4, d), lambda b, h: (0, h)),
                  pl.BlockSpec((1, d), lambda b, h: (0, h))],
        out_specs=pl.BlockSpec((seq, d), lambda b, h: (b, h)),
        out_shape=jax.ShapeDtypeStruct((n, heads * d), BF16),
        scratch_shapes=[pltpu.VMEM((seq, d), F32), pltpu.VMEM((2, seq, d), F32), pltpu.VMEM((2, seq, d), F32),
                        pltpu.VMEM((2, seq, d), F32)],
        compiler_params=pltpu.CompilerParams(
            dimension_semantics=("parallel", "parallel"),
            vmem_limit_bytes=_vmem_limit(block_bytes, scratch_bytes)),
        name="hgrn",
    )(zq, z, z, z, z, lbp, norm_g)


def _qk_conv_kernel(x_ref, w_ref, b_ref, o_ref, xp_ref, *, tile, q_blocks, q_scale):
    t, c = x_ref.shape
    pad = V7X_SUBLANES
    half = CONV_WIDTH // 2
    xp_ref[0:pad, :] = jnp.zeros((pad, c), F32)
    xp_ref[pad + t:2 * pad + t, :] = jnp.zeros((pad, c), F32)
    xp_ref[pad:pad + t, :] = x_ref[...]
    w = w_ref[...]
    bias = b_ref[...]
    scale = jnp.where(pl.program_id(1) < q_blocks, q_scale, 1.0)
    win = tile + 2 * pad

    def body(i, carry):
        r0 = pl.multiple_of(i * tile, tile)
        window = xp_ref[pl.ds(r0, win), :]
        acc = bias + w[half:half + 1, :] * window[pad:pad + tile, :]
        for j in range(CONV_WIDTH):
            off = j - half
            if off == 0:
                continue
            shifted = pltpu.roll(window, (win - off) % win, 0)
            acc = acc + w[j:j + 1, :] * shifted[pad:pad + tile, :]
        o_ref[pl.ds(r0, tile), :] = (acc * _sigmoid(acc) * scale).astype(o_ref.dtype)
        return carry

    lax.fori_loop(0, t // tile, body, 0)


def _qk_conv(z, conv_w, conv_b, *, batch, seq, col_block0, width, q_width, q_scale, cblk=256, tile=256):
    n = z.shape[0]
    assert seq % tile == 0 and width % cblk == 0 and q_width % cblk == 0
    block_bytes = seq * cblk * 4 + CONV_WIDTH * cblk * 4 + cblk * 4 + seq * cblk * 2
    scratch_bytes = (seq + 2 * V7X_SUBLANES) * cblk * 4
    return pl.pallas_call(
        functools.partial(_qk_conv_kernel, tile=tile, q_blocks=q_width // cblk, q_scale=q_scale),
        grid=(batch, width // cblk),
        in_specs=[pl.BlockSpec((seq, cblk), lambda b, j: (b, col_block0 + j)),
                  pl.BlockSpec((CONV_WIDTH, cblk), lambda b, j: (0, j)),
                  pl.BlockSpec((1, cblk), lambda b, j: (0, j))],
        out_specs=pl.BlockSpec((seq, cblk), lambda b, j: (b, j)),
        out_shape=jax.ShapeDtypeStruct((n, width), BF16),
        scratch_shapes=[pltpu.VMEM((seq + 2 * V7X_SUBLANES, cblk), F32)],
        compiler_params=pltpu.CompilerParams(
            dimension_semantics=("parallel", "parallel"),
            vmem_limit_bytes=_vmem_limit(block_bytes, scratch_bytes)),
        name="qk_conv",
    )(z, conv_w, conv_b)


def _mlstm_chunk(q, k, v, lf_c, ig_c, lf_r, ig_r, c_ref, n, m, *, reverse):
    c = q.shape[0]
    row = lax.broadcasted_iota(jnp.int32, (c, c), 0)
    col = lax.broadcasted_iota(jnp.int32, (c, c), 1)
    tri = (col >= row) if reverse else (col <= row)
    tri_t = (row >= col) if reverse else (row <= col)
    bc = jnp.sum(jnp.where(tri, lf_r, 0.0), axis=1, keepdims=True)
    br = jnp.sum(jnp.where(tri_t, lf_c, 0.0), axis=0, keepdims=True)
    src = jnp.where(tri, ig_r - br, -jnp.inf)
    peak = jnp.maximum(m, jnp.max(src, axis=1, keepdims=True))
    inter = jnp.exp2(m - peak)
    sc = _nt(q, k) * jnp.exp2(src - peak)
    cst = c_ref[...]
    num = (jnp.dot(sc.astype(BF16), v, preferred_element_type=F32)
           + inter * jnp.dot(q, cst.astype(BF16), preferred_element_type=F32))
    qn = jnp.sum(q.astype(F32) * n, axis=1, keepdims=True)
    den = jnp.sum(sc, axis=1, keepdims=True) + inter * qn
    h = num / jnp.maximum(jnp.abs(den), jnp.exp2(-(bc + peak)))

    bend = bc[0:1, :] if reverse else bc[c - 1:c, :]
    w_c = bend - bc + ig_c
    m_new = jnp.maximum(bend + m, jnp.max(w_c, axis=0, keepdims=True))
    carry_scale = jnp.exp2(bend + m - m_new)
    kw = k.astype(F32) * jnp.exp2(w_c - m_new)
    c_ref[...] = carry_scale * cst + _tn(kw.astype(BF16), v)
    n_new = carry_scale * n + jnp.sum(kw, axis=0, keepdims=True)
    return h, n_new, m_new


def _mlstm_kernel(q_ref, k_ref, v_ref, og_ref, gc_ref, gr_ref, bias_r_ref, bias_c_ref, ng_ref,
                  y_ref, h_ref, c_ref, *, chunk):
    t, d = q_ref.shape
    nc = t // chunk
    bias_r = bias_r_ref[...]
    bias_c = bias_c_ref[...]

    def chunk_rows(cidx):
        return pl.ds(pl.multiple_of(cidx * chunk, chunk), chunk)

    def clear(cidx, carry):
        h_ref[chunk_rows(cidx), :] = jnp.zeros((chunk, d), F32)
        return carry

    lax.fori_loop(0, nc, clear, 0, unroll=4)
    c_ref[...] = jnp.zeros_like(c_ref)

    def scan(i, carry):
        new_carry = []
        for direction in (0, 1):
            n, m = carry[direction]
            rows = chunk_rows(i if direction == 0 else nc - 1 - i)
            gc = gc_ref[rows, :] + bias_r
            gr = gr_ref[:, rows] + bias_c
            ig_c = gc[:, direction:direction + 1] * LOG2E
            lf_c = _log_sigmoid(gc[:, 2 + direction:3 + direction]) * LOG2E
            ig_r = gr[direction:direction + 1, :] * LOG2E
            lf_r = _log_sigmoid(gr[2 + direction:3 + direction, :]) * LOG2E
            h, n, m = _mlstm_chunk(q_ref[rows, :], k_ref[rows, :], v_ref[rows, :].astype(BF16),
                                   lf_c, ig_c, lf_r, ig_r, c_ref.at[direction], n, m,
                                   reverse=direction == 1)
            h_ref[rows, :] += h
            new_carry.append((n, m))
        return tuple(new_carry)

    init = (jnp.zeros((1, d), F32), jnp.full((1, 1), M_INIT * LOG2E, F32))
    lax.fori_loop(0, nc, scan, (init, init), unroll=2)

    ng = ng_ref[...]

    def finish(cidx, carry):
        rows = chunk_rows(cidx)
        h = h_ref[rows, :]
        mu = jnp.mean(h, axis=-1, keepdims=True)
        hc = h - mu
        var = jnp.mean(hc * hc, axis=-1, keepdims=True)
        y = hc * lax.rsqrt(var + NORM_EPS) * ng * _sigmoid(og_ref[rows, :])
        y_ref[rows, :] = y.astype(y_ref.dtype)
        return carry

    lax.fori_loop(0, nc, finish, 0, unroll=4)


def _mlstm(qk, z, gcol, grow, bias_r, bias_c, norm_g, *, batch, seq, heads, d, v_block0, o_block0,
           chunk=4 * SCAN_CHUNK):
    n = z.shape[0]
    assert n == batch * seq and seq % (2 * chunk) == 0
    block_bytes = 2 * seq * d * 2 + 2 * seq * d * 4 + 2 * seq * V7X_LANES * 4 + seq * d * 2
    scratch_bytes = seq * d * 4 + 2 * d * d * 4
    return pl.pallas_call(
        functools.partial(_mlstm_kernel, chunk=chunk),
        grid=(batch, heads),
        in_specs=[
            pl.BlockSpec((seq, d), lambda b, h: (b, h)),
            pl.BlockSpec((seq, d), lambda b, h: (b, heads + h)),
            pl.BlockSpec((seq, d), lambda b, h: (b, v_block0 + h)),
            pl.BlockSpec((seq, d), lambda b, h: (b, o_block0 + h)),
            pl.BlockSpec((None, None, seq, 4), lambda b, h: (b, h, 0, 0)),
            pl.BlockSpec((None, None, 4, seq), lambda b, h: (b, h, 0, 0)),
            pl.BlockSpec((None, 1, 4), lambda b, h: (h, 0, 0)),
            pl.BlockSpec((None, 4, 1), lambda b, h: (h, 0, 0)),
            pl.BlockSpec((1, d), lambda b, h: (0, h)),
        ],
        out_specs=pl.BlockSpec((seq, d), lambda b, h: (b, h)),
        out_shape=jax.ShapeDtypeStruct((n, heads * d), BF16),
        scratch_shapes=[pltpu.VMEM((seq, d), F32), pltpu.VMEM((2, d, d), F32)],
        compiler_params=pltpu.CompilerParams(
            dimension_semantics=("parallel", "parallel"),
            vmem_limit_bytes=_vmem_limit(block_bytes, scratch_bytes)),
        name="mlstm",
    )(qk, qk, z, z, gcol, grow, bias_r, bias_c, norm_g)


def _out_proj_kernel(x_ref, yh_ref, ym_ref, wh_ref, wm_ref, g_ref, b_ref, o_ref, whbf_ref, wmbf_ref, *, alpha):
    @pl.when(pl.program_id(0) == 0)
    def _():
        whbf_ref[...] = wh_ref[...].astype(BF16)
        wmbf_ref[...] = wm_ref[...].astype(BF16)

    r = (x_ref[...] * alpha
         + jnp.dot(yh_ref[...], whbf_ref[...], preferred_element_type=F32)
         + jnp.dot(ym_ref[...], wmbf_ref[...], preferred_element_type=F32))
    mu = jnp.mean(r, axis=-1, keepdims=True)
    d = r - mu
    var = jnp.mean(d * d, axis=-1, keepdims=True)
    o_ref[...] = d * lax.rsqrt(var + LN_EPS) * g_ref[...] + b_ref[...]


def _out_proj(x, yh, ym, w, g, b, *, alpha, tm=512):
    n, d = x.shape
    kh, km = yh.shape[1], ym.shape[1]
    assert n % tm == 0 and kh == km and w.shape[0] == kh + km
    block_bytes = tm * d * 4 + tm * kh * 2 + tm * km * 2 + tm * d * 4
    weight_bytes = (kh + km) * d * 4
    return pl.pallas_call(
        functools.partial(_out_proj_kernel, alpha=alpha),
        grid=(n // tm,),
        in_specs=[
            pl.BlockSpec((tm, d), lambda i: (i, 0)),
            pl.BlockSpec((tm, kh), lambda i: (i, 0)),
            pl.BlockSpec((tm, km), lambda i: (i, 0)),
            pl.BlockSpec((kh, d), lambda i: (0, 0), pipeline_mode=pl.Buffered(1)),
            pl.BlockSpec((km, d), lambda i: (1, 0), pipeline_mode=pl.Buffered(1)),
            pl.BlockSpec((1, d), lambda i: (0, 0)),
            pl.BlockSpec((1, d), lambda i: (0, 0)),
        ],
        out_specs=pl.BlockSpec((tm, d), lambda i: (i, 0)),
        out_shape=jax.ShapeDtypeStruct((n, d), F32),
        scratch_shapes=[pltpu.VMEM((kh, d), BF16), pltpu.VMEM((km, d), BF16)],
        compiler_params=pltpu.CompilerParams(
            dimension_semantics=("arbitrary",),
            vmem_limit_bytes=_vmem_limit(block_bytes, weight_bytes + weight_bytes // 2)),
        name="out_proj",
    )(x, yh, ym, w, w, g, b)


def kernel(x, ffn1_w1, ffn1_w3, ffn1_w2, ln1_g, ln1_b, w_in, hgrn_lb, hgrn_norm_g, mlstm_conv_w,
           mlstm_conv_b, mlstm_ig_b, mlstm_fg_b, mlstm_norm_g, w_out, ln2_g, ln2_b, ffn2_w1, ffn2_w3,
           ffn2_w2, ln3_g, ln3_b):
    batch, seq, d_model = x.shape
    depth = ffn1_w1.shape[0]
    assert depth == 1 and hgrn_lb.shape[1] == 2
    alpha = (2.0 * depth) ** 0.25
    hgrn_width = hgrn_norm_g.shape[1]
    mlstm_width = mlstm_norm_g.shape[1]
    hgrn_heads = hgrn_width // HGRN_HEAD_DIM
    md = mlstm_width // MLSTM_HEADS
    main_cols = 5 * hgrn_width + 4 * mlstm_width
    n = batch * seq

    def row(p):
        return p.reshape(1, -1).astype(F32)

    xf = x.reshape(n, d_model)
    x1 = _ffn_ln(xf, ffn1_w1[0], ffn1_w3[0], ffn1_w2[0], row(ln1_g[0]), row(ln1_b[0]), alpha=alpha)

    n_gates = w_in.shape[2] - main_cols
    zq, z, gates = _in_proj(x1, w_in[0].T, cols=main_cols, tn=hgrn_width)
    mlstm_col0 = 4 * hgrn_width

    y_h = _hgrn(zq, z, hgrn_lb.reshape(4, hgrn_width).astype(F32), row(hgrn_norm_g[0]),
                batch=batch, seq=seq, heads=hgrn_heads)

    qk = _qk_conv(z, mlstm_conv_w[0].astype(F32), row(mlstm_conv_b[0]), batch=batch, seq=seq,
                  col_block0=mlstm_col0 // 256, width=2 * mlstm_width, q_width=mlstm_width,
                  q_scale=md ** -0.5)

    g4 = gates[:, :n_gates].reshape(batch, seq, 4, MLSTM_HEADS)
    gcol = g4.transpose(0, 3, 1, 2)
    grow = g4.transpose(0, 3, 2, 1)
    bias4 = jnp.stack([mlstm_ig_b[0, 0], mlstm_ig_b[0, 1], mlstm_fg_b[0, 0], mlstm_fg_b[0, 1]],
                      axis=-1).astype(F32)
    y_m = _mlstm(qk, z, gcol, grow, bias4[:, None, :], bias4[:, :, None], row(mlstm_norm_g[0]),
                 batch=batch, seq=seq, heads=MLSTM_HEADS, d=md,
                 v_block0=(mlstm_col0 + 2 * mlstm_width) // md,
                 o_block0=(mlstm_col0 + 3 * mlstm_width) // md)

    x2 = _out_proj(x1, y_h, y_m, w_out[0], row(ln2_g[0]), row(ln2_b[0]), alpha=alpha)
    x3 = _ffn_ln(x2, ffn2_w1[0], ffn2_w3[0], ffn2_w2[0], row(ln3_g[0]), row(ln3_b[0]), alpha=alpha)
    return x3.reshape(batch, seq, d_model)
```

```python
import functools

import jax
import jax.numpy as jnp
from jax import lax
from jax.experimental import pallas as pl
from jax.experimental.pallas import tpu as pltpu

F32 = jnp.float32
BF16 = jnp.bfloat16

V7X_VMEM_BYTES = 64 * 1024 * 1024
V7X_LANES = 128
V7X_SUBLANES = 8

HGRN_HEAD_DIM = 128
MLSTM_HEADS = 4
CONV_WIDTH = 5
LN_EPS = 1e-5
NORM_EPS = 1e-6
M_INIT = -1e30
SCAN_CHUNK = 128


def _vmem_limit(block_bytes, scratch_bytes):
    need = 2 * block_bytes + scratch_bytes
    return int(min(V7X_VMEM_BYTES - (4 << 20), need + (16 << 20)))


def _nt(a, b):
    return lax.dot_general(a, b, (((1,), (1,)), ((), ())), preferred_element_type=F32)


def _tn(a, b):
    return lax.dot_general(a, b, (((0,), (0,)), ((), ())), preferred_element_type=F32)


def _sigmoid(x):
    return 1.0 / (1.0 + jnp.exp(-x))


def _log_sigmoid(x):
    return -(jnp.maximum(-x, 0.0) + jnp.log(1.0 + jnp.exp(-jnp.abs(x))))


def _ffn_ln_kernel(x_ref, w1_ref, w3_ref, w2_ref, g_ref, b_ref, o_ref, xbf_ref, *, alpha):
    j = pl.program_id(1)

    @pl.when(j == 0)
    def _():
        xbf_ref[...] = x_ref[...].astype(BF16)
        o_ref[...] = jnp.zeros_like(o_ref)

    xb = xbf_ref[...]
    h1 = jnp.dot(xb, w1_ref[...].astype(BF16), preferred_element_type=F32)
    h3 = jnp.dot(xb, w3_ref[...].astype(BF16), preferred_element_type=F32)
    gate = (h1 * _sigmoid(h1) * h3).astype(BF16)
    o_ref[...] += jnp.dot(gate, w2_ref[...].astype(BF16), preferred_element_type=F32)

    @pl.when(j == pl.num_programs(1) - 1)
    def _():
        r = x_ref[...] * alpha + 0.5 * o_ref[...]
        mu = jnp.mean(r, axis=-1, keepdims=True)
        d = r - mu
        var = jnp.mean(d * d, axis=-1, keepdims=True)
        o_ref[...] = d * lax.rsqrt(var + LN_EPS) * g_ref[...] + b_ref[...]


def _ffn_ln(x, w1, w3, w2, g, b, *, alpha, tm=1024, tf=256):
    n, d = x.shape
    f = w1.shape[1]
    assert n % tm == 0 and f % tf == 0
    block_bytes = tm * d * 4 + 3 * d * tf * 4 + tm * d * 4
    scratch_bytes = tm * d * 2 + 3 * d * tf * 2 + 3 * tm * tf * 4
    return pl.pallas_call(
        functools.partial(_ffn_ln_kernel, alpha=alpha),
        grid=(n // tm, f // tf),
        in_specs=[
            pl.BlockSpec((tm, d), lambda i, j: (i, 0)),
            pl.BlockSpec((d, tf), lambda i, j: (0, j)),
            pl.BlockSpec((d, tf), lambda i, j: (0, j)),
            pl.BlockSpec((tf, d), lambda i, j: (j, 0)),
            pl.BlockSpec((1, d), lambda i, j: (0, 0)),
            pl.BlockSpec((1, d), lambda i, j: (0, 0)),
        ],
        out_specs=pl.BlockSpec((tm, d), lambda i, j: (i, 0)),
        out_shape=jax.ShapeDtypeStruct((n, d), F32),
        scratch_shapes=[pltpu.VMEM((tm, d), BF16)],
        compiler_params=pltpu.CompilerParams(
            dimension_semantics=("parallel", "arbitrary"),
            vmem_limit_bytes=_vmem_limit(block_bytes, scratch_bytes)),
        name="ffn_ln",
    )(x, w1, w3, w2, g, b)


def _in_proj_head_kernel(x_ref, wt_ref, wgt_ref, z_ref, gates_ref, xbf_ref, wbf_ref, wgbf_ref):
    @pl.when(pl.program_id(0) == 0)
    def _():
        wbf_ref[...] = wt_ref[...].astype(BF16)
        wgbf_ref[...] = wgt_ref[...].astype(BF16)

    xb = x_ref[...].astype(BF16)
    xbf_ref[...] = xb
    z_ref[...] = _nt(xb, wbf_ref[...])
    gates_ref[...] = _nt(xb, wgbf_ref[...])


def _in_proj_tail_kernel(xbf_ref, wt_ref, z_ref, wbf_ref):
    @pl.when(pl.program_id(1) == 0)
    def _():
        wbf_ref[...] = wt_ref[...].astype(BF16)

    z_ref[...] = _nt(xbf_ref[...], wbf_ref[...])


def _in_proj(x, wt, *, cols, tm=1024, tn=1024):
    n, d = x.shape
    gl = wt.shape[0] - cols
    assert n % tm == 0 and cols % tn == 0 and cols > tn
    assert gl % V7X_SUBLANES == 0 and cols % gl == 0
    head_blocks = tm * d * 4 + tm * tn * 4 + tm * V7X_LANES * 4 + tm * d * 2
    head_weights = tn * d * 4 + gl * d * 4
    z_head, gates, xbf = pl.pallas_call(
        _in_proj_head_kernel,
        grid=(n // tm,),
        in_specs=[
            pl.BlockSpec((tm, d), lambda i: (i, 0)),
            pl.BlockSpec((tn, d), lambda i: (0, 0), pipeline_mode=pl.Buffered(1)),
            pl.BlockSpec((gl, d), lambda i: (cols // gl, 0), pipeline_mode=pl.Buffered(1)),
        ],
        out_specs=[
            pl.BlockSpec((tm, tn), lambda i: (i, 0)),
            pl.BlockSpec((tm, gl), lambda i: (i, 0)),
            pl.BlockSpec((tm, d), lambda i: (i, 0)),
        ],
        out_shape=[jax.ShapeDtypeStruct((n, tn), F32), jax.ShapeDtypeStruct((n, gl), F32),
                   jax.ShapeDtypeStruct((n, d), BF16)],
        scratch_shapes=[pltpu.VMEM((tn, d), BF16), pltpu.VMEM((gl, d), BF16)],
        compiler_params=pltpu.CompilerParams(
            dimension_semantics=("arbitrary",),
            vmem_limit_bytes=_vmem_limit(head_blocks, head_weights + head_weights // 2)),
        name="in_proj_head",
    )(x, wt, wt)

    tail_blocks = tm * d * 2 + tn * d * 4 + tm * tn * 4
    z_tail = pl.pallas_call(
        _in_proj_tail_kernel,
        grid=(cols // tn - 1, n // tm),
        in_specs=[
            pl.BlockSpec((tm, d), lambda j, i: (i, 0)),
            pl.BlockSpec((tn, d), lambda j, i: (j + 1, 0)),
        ],
        out_specs=pl.BlockSpec((tm, tn), lambda j, i: (i, j)),
        out_shape=jax.ShapeDtypeStruct((n, cols - tn), F32),
        scratch_shapes=[pltpu.VMEM((tn, d), BF16)],
        compiler_params=pltpu.CompilerParams(
            dimension_semantics=("parallel", "arbitrary"),
            vmem_limit_bytes=_vmem_limit(tail_blocks, tn * d * 2)),
        name="in_proj_tail",
    )(xbf, wt)
    return z_head, z_tail, gates


LOG2E = 1.4426950408889634
HGRN_FLAT_BLOCK = 128
HGRN_FLAT_LOG2_RANGE = 100.0


def _cumsum01(tri_bf, g):
    g1 = g.astype(BF16)
    r1 = g - g1.astype(F32)
    g2 = r1.astype(BF16)
    g3 = (r1 - g2.astype(F32)).astype(BF16)
    return (jnp.dot(tri_bf, g1, preferred_element_type=F32)
            + jnp.dot(tri_bf, g2, preferred_element_type=F32)
            + jnp.dot(tri_bf, g3, preferred_element_type=F32))


def _neg_abs(x):
    bits = lax.bitcast_convert_type(x, jnp.uint32) | jnp.uint32(0x80000000)
    return lax.bitcast_convert_type(bits, F32)


def _hgrn_level_operand(q, k, f, b, m, *, reverse):
    c, d = q.shape
    sub = V7X_SUBLANES
    if m >= sub:
        refs, sides = [], []
        for p in range(c // (2 * m)):
            r = p * 2 * m + (m if reverse else m - 1)
            refs.append(jnp.broadcast_to(b[r:r + 1, :], (2 * m, d)))
            lo, hi = slice(p * 2 * m, p * 2 * m + m), slice(p * 2 * m + m, (p + 1) * 2 * m)
            sides += [q[lo], k[hi]] if reverse else [k[lo], q[hi]]
        ref = refs[0] if len(refs) == 1 else jnp.concatenate(refs, axis=0)
        y0 = jnp.concatenate(sides, axis=0)
        return y0 * jnp.exp2(_neg_abs(b - ref))
    nb = c // sub
    s_idx = lax.broadcasted_iota(jnp.int32, (1, sub, d), 1)
    q_side = ((s_idx & m) == 0) if reverse else ((s_idx & m) != 0)
    q3, k3 = q.reshape(nb, sub, d), k.reshape(nb, sub, d)
    if m == 1:
        y3 = jnp.where(q_side, q3 * f.reshape(nb, sub, d), k3)
        return y3.reshape(c, d)
    b3 = b.reshape(nb, sub, d)
    r0 = m if reverse else m - 1
    ref3 = jnp.broadcast_to(b3[:, r0:r0 + 1, :], (nb, sub, d))
    for r in range(r0 + 2 * m, sub, 2 * m):
        ref3 = jnp.where(s_idx < r - r0, ref3, jnp.broadcast_to(b3[:, r:r + 1, :], (nb, sub, d)))
    y3 = jnp.where(q_side, q3, k3) * jnp.exp2(_neg_abs(b3 - ref3))
    return y3.reshape(c, d)


def _hgrn_flat_operands(q, k, b, block, *, reverse):
    c, d = q.shape
    refs = []
    for p in range(c // block):
        r = p * block + (block - 1 if reverse else 0)
        refs.append(jnp.broadcast_to(b[r:r + 1, :], (block, d)))
    e = b - jnp.concatenate(refs, axis=0)
    return q * jnp.exp2(e), k * jnp.exp2(-e)


def _hgrn_chunk(q, k, v, f, b, st, level_map, flat_mask, *, reverse, flat_block):
    c, d = q.shape
    bend = b[0:1, :] if reverse else b[c - 1:c, :]
    qd = (q * jnp.exp2(b)).astype(BF16)
    kd = (k * jnp.exp2(bend - b)).astype(BF16)
    vb = v.astype(BF16)
    if flat_block:
        o = _nt(qd, st.astype(BF16))
        st_new = st * jnp.exp2(bend) + _tn(vb, kd)
        yq, yk = _hgrn_flat_operands(q, k, b, flat_block, reverse=reverse)
        scores = jnp.dot(yq.astype(BF16), yk.T.astype(BF16), preferred_element_type=F32)
        attn = jnp.where(flat_mask, scores, 0.0)
        m, level = flat_block, flat_block.bit_length() - 1
    else:
        o = jnp.dot(qd, st.astype(BF16), preferred_element_type=F32)
        decay_col = jnp.broadcast_to(jnp.exp2(bend), (d, d)).T
        st_new = st * decay_col + _tn(kd, vb)
        attn = jnp.zeros((c, c), F32)
        m, level = 1, 0
    while m < c:
        y = _hgrn_level_operand(q, k, f, b, m, reverse=reverse)
        scores = jnp.dot(y.astype(BF16), y.T.astype(BF16), preferred_element_type=F32)
        attn = jnp.where(level_map == level, scores, attn)
        m, level = 2 * m, level + 1
    o = o + jnp.dot(attn.astype(BF16), vb, preferred_element_type=F32)
    if not flat_block:
        o = o + jnp.sum(q * k, axis=-1, keepdims=True) * v
    return o, st_new


def _hgrn_kernel(zq_ref, zi_ref, zg_ref, zff_ref, zfb_ref, lbp_ref, ng_ref, y_ref,
                 q_ref, f_ref, b_ref, o_ref, *, chunk):
    t, d = zq_ref.shape
    nc = t // chunk
    lbp = lbp_ref[...]

    def lower_bound(a0, a1):
        mx = jnp.maximum(a0, a1)
        e0 = jnp.exp(a0 - mx)
        e1 = jnp.exp(a1 - mx)
        return e0 / (e0 + e1)

    lbs = (lower_bound(lbp[0:1, :], lbp[1:2, :]), lower_bound(lbp[2:3, :], lbp[3:4, :]))
    zf_refs = (zff_ref, zfb_ref)
    scale = HGRN_HEAD_DIM ** -0.5

    row = lax.broadcasted_iota(jnp.int32, (chunk, chunk), 0)
    col = lax.broadcasted_iota(jnp.int32, (chunk, chunk), 1)

    def chunk_rows(cidx):
        return pl.ds(pl.multiple_of(cidx * chunk, chunk), chunk)

    tris = (jnp.where(col <= row, 1.0, 0.0).astype(BF16), jnp.where(col >= row, 1.0, 0.0).astype(BF16))

    def prepare(cidx, spread):
        rows = chunk_rows(cidx)
        zq = zq_ref[rows, :]
        q_ref[rows, :] = zq * _sigmoid(zq) * scale
        for direction in (0, 1):
            lb = lbs[direction]
            f = lb + (1.0 - lb) * _sigmoid(zf_refs[direction][rows, :])
            f_ref[direction, rows, :] = f
            b = _cumsum01(tris[direction], jnp.log(f) * LOG2E)
            b_ref[direction, rows, :] = b
            for r0 in range(0, chunk, HGRN_FLAT_BLOCK):
                spread = jnp.maximum(spread, jnp.abs(b[r0:r0 + 1, :] - b[r0 + HGRN_FLAT_BLOCK - 1:r0 + HGRN_FLAT_BLOCK, :]))
        return spread

    spread = lax.fori_loop(0, nc, prepare, jnp.zeros((1, d), F32), unroll=8)
    bounded = jnp.max(spread) <= HGRN_FLAT_LOG2_RANGE

    differ = row ^ col
    top_bit = jnp.zeros((chunk, chunk), jnp.int32)
    m = 2
    while m < chunk:
        top_bit = top_bit + jnp.where(differ >= m, 1, 0)
        m *= 2
    level_maps = (jnp.where(row > col, top_bit, -1), jnp.where(col > row, top_bit, -1))
    same_block = (row // HGRN_FLAT_BLOCK) == (col // HGRN_FLAT_BLOCK)
    flat_masks = (same_block & (row >= col), same_block & (row <= col))

    def run_scan(flat_block):
        def scan(i, states):
            new_states = []
            for direction in (0, 1):
                rows = chunk_rows(i if direction == 0 else nc - 1 - i)
                f = f_ref[direction, rows, :]
                o, st_new = _hgrn_chunk(q_ref[rows, :], 1.0 - f, zi_ref[rows, :], f, b_ref[direction, rows, :],
                                        states[direction], level_maps[direction], flat_masks[direction],
                                        reverse=direction == 1, flat_block=flat_block)
                new_states.append(st_new)
                o_ref[direction, rows, :] = o
            return tuple(new_states)

        lax.fori_loop(0, nc, scan, (jnp.zeros((d, d), F32), jnp.zeros((d, d), F32)),
                      unroll=8 if flat_block else 4)
        return 0

    lax.cond(bounded, lambda: run_scan(HGRN_FLAT_BLOCK), lambda: run_scan(0))

    ng = ng_ref[...]

    def finish(cidx, carry):
        rows = chunk_rows(cidx)
        o = o_ref[0, rows, :] + o_ref[1, rows, :]
        o = o * lax.rsqrt(jnp.mean(o * o, axis=-1, keepdims=True) + NORM_EPS)
        zg = zg_ref[rows, :]
        y_ref[rows, :] = (o * ng * (zg * _sigmoid(zg))).astype(y_ref.dtype)
        return carry

    lax.fori_loop(0, nc, finish, 0, unroll=4)


def _hgrn(zq, z, lbp, norm_g, *, batch, seq, heads, chunk=SCAN_CHUNK):
    d = HGRN_HEAD_DIM
    n = zq.shape[0]
    assert n == batch * seq and seq % (2 * chunk) == 0
    assert zq.shape[1] == heads * d

    def zspec(off):
        return pl.BlockSpec((seq, d), lambda b, h: (b, off * heads + h))

    block_bytes = 5 * seq * d * 4 + seq * d * 2
    scratch_bytes = 7 * seq * d * 4
    return pl.pallas_call(
        functools.partial(_hgrn_kernel, chunk=chunk),
        grid=(batch, heads),
        in_specs=[zspec(0), zspec(0), zspec(1), zspec(2), zspec(3),
                  pl.BlockSpec((4, d), lambda b, h: (0, h)),
                  pl.BlockSpec((1, d), lambda b, h: (0, h))],
        out_specs=pl.BlockSpec((seq, d), lambda b, h: (b, h)),
        out_shape=jax.ShapeDtypeStruct((n, heads * d), BF16),
        scratch_shapes=[pltpu.VMEM((seq, d), F32), pltpu.VMEM((2, seq, d), F32), pltpu.VMEM((2, seq, d), F32),
                        pltpu.VMEM((2, seq, d), F32)],
        compiler_params=pltpu.CompilerParams(
            dimension_semantics=("parallel", "parallel"),
            vmem_limit_bytes=_vmem_limit(block_bytes, scratch_bytes)),
        name="hgrn",
    )(zq, z, z, z, z, lbp, norm_g)


def _qk_conv_kernel(x_ref, w_ref, b_ref, o_ref, xp_ref, *, tile, q_blocks, q_scale):
    t, c = x_ref.shape
    pad = V7X_SUBLANES
    half = CONV_WIDTH // 2
    xp_ref[0:pad, :] = jnp.zeros((pad, c), F32)
    xp_ref[pad + t:2 * pad + t, :] = jnp.zeros((pad, c), F32)
    xp_ref[pad:pad + t, :] = x_ref[...]
    w = w_ref[...]
    bias = b_ref[...]
    scale = jnp.where(pl.program_id(1) < q_blocks, q_scale, 1.0)
    win = tile + 2 * pad

    def body(i, carry):
        r0 = pl.multiple_of(i * tile, tile)
        window = xp_ref[pl.ds(r0, win), :]
        acc = bias + w[half:half + 1, :] * window[pad:pad + tile, :]
        for j in range(CONV_WIDTH):
            off = j - half
            if off == 0:
                continue
            shifted = pltpu.roll(window, (win - off) % win, 0)
            acc = acc + w[j:j + 1, :] * shifted[pad:pad + tile, :]
        o_ref[pl.ds(r0, tile), :] = (acc * _sigmoid(acc) * scale).astype(o_ref.dtype)
        return carry

    lax.fori_loop(0, t // tile, body, 0)


def _qk_conv(z, conv_w, conv_b, *, batch, seq, col_block0, width, q_width, q_scale, cblk=256, tile=256):
    n = z.shape[0]
    assert seq % tile == 0 and width % cblk == 0 and q_width % cblk == 0
    block_bytes = seq * cblk * 4 + CONV_WIDTH * cblk * 4 + cblk * 4 + seq * cblk * 2
    scratch_bytes = (seq + 2 * V7X_SUBLANES) * cblk * 4
    return pl.pallas_call(
        functools.partial(_qk_conv_kernel, tile=tile, q_blocks=q_width // cblk, q_scale=q_scale),
        grid=(batch, width // cblk),
        in_specs=[pl.BlockSpec((seq, cblk), lambda b, j: (b, col_block0 + j)),
                  pl.BlockSpec((CONV_WIDTH, cblk), lambda b, j: (0, j)),
                  pl.BlockSpec((1, cblk), lambda b, j: (0, j))],
        out_specs=pl.BlockSpec((seq, cblk), lambda b, j: (b, j)),
        out_shape=jax.ShapeDtypeStruct((n, width), BF16),
        scratch_shapes=[pltpu.VMEM((seq + 2 * V7X_SUBLANES, cblk), F32)],
        compiler_params=pltpu.CompilerParams(
            dimension_semantics=("parallel", "parallel"),
            vmem_limit_bytes=_vmem_limit(block_bytes, scratch_bytes)),
        name="qk_conv",
    )(z, conv_w, conv_b)


def _mlstm_chunk(q, k, v, lf_c, ig_c, lf_r, ig_r, c_ref, n, m, *, reverse):
    c = q.shape[0]
    row = lax.broadcasted_iota(jnp.int32, (c, c), 0)
    col = lax.broadcasted_iota(jnp.int32, (c, c), 1)
    tri = (col >= row) if reverse else (col <= row)
    tri_t = (row >= col) if reverse else (row <= col)
    bc = jnp.sum(jnp.where(tri, lf_r, 0.0), axis=1, keepdims=True)
    br = jnp.sum(jnp.where(tri_t, lf_c, 0.0), axis=0, keepdims=True)
    dmat = jnp.where(tri, bc - br + ig_r, -jnp.inf)
    m_inter = bc + m
    m_t = jnp.maximum(m_inter, jnp.max(dmat, axis=1, keepdims=True))
    inter = jnp.exp(m_inter - m_t)
    sc = _nt(q, k) * jnp.exp(dmat - m_t)
    cst = c_ref[...]
    num = (jnp.dot(sc.astype(BF16), v, preferred_element_type=F32)
           + inter * jnp.dot(q, cst.astype(BF16), preferred_element_type=F32))
    qn = jnp.sum(q.astype(F32) * n, axis=1, keepdims=True)
    den = jnp.sum(sc, axis=1, keepdims=True) + inter * qn
    h = num / jnp.maximum(jnp.abs(den), jnp.exp(-m_t))

    bend = bc[0:1, :] if reverse else bc[c - 1:c, :]
    w_c = bend - bc + ig_c
    m_new = jnp.maximum(bend + m, jnp.max(w_c, axis=0, keepdims=True))
    carry_scale = jnp.exp(bend + m - m_new)
    kw = k.astype(F32) * jnp.exp(w_c - m_new)
    c_ref[...] = carry_scale * cst + _tn(kw.astype(BF16), v)
    n_new = carry_scale * n + jnp.sum(kw, axis=0, keepdims=True)
    return h, n_new, m_new


def _mlstm_kernel(q_ref, k_ref, v_ref, og_ref, gc_ref, gr_ref, bias_r_ref, bias_c_ref, ng_ref,
                  y_ref, h_ref, c_ref, *, chunk):
    t, d = q_ref.shape
    nc = t // chunk
    bias_r = bias_r_ref[...]
    bias_c = bias_c_ref[...]

    def chunk_rows(cidx):
        return pl.ds(pl.multiple_of(cidx * chunk, chunk), chunk)

    def clear(cidx, carry):
        h_ref[chunk_rows(cidx), :] = jnp.zeros((chunk, d), F32)
        return carry

    lax.fori_loop(0, nc, clear, 0, unroll=4)
    c_ref[...] = jnp.zeros_like(c_ref)

    def scan(i, carry):
        new_carry = []
        for direction in (0, 1):
            n, m = carry[direction]
            rows = chunk_rows(i if direction == 0 else nc - 1 - i)
            gc = gc_ref[rows, :] + bias_r
            gr = gr_ref[:, rows] + bias_c
            ig_c = gc[:, direction:direction + 1]
            lf_c = _log_sigmoid(gc[:, 2 + direction:3 + direction])
            ig_r = gr[direction:direction + 1, :]
            lf_r = _log_sigmoid(gr[2 + direction:3 + direction, :])
            h, n, m = _mlstm_chunk(q_ref[rows, :], k_ref[rows, :], v_ref[rows, :].astype(BF16),
                                   lf_c, ig_c, lf_r, ig_r, c_ref.at[direction], n, m,
                                   reverse=direction == 1)
            h_ref[rows, :] += h
            new_carry.append((n, m))
        return tuple(new_carry)

    init = (jnp.zeros((1, d), F32), jnp.full((1, 1), M_INIT, F32))
    lax.fori_loop(0, nc, scan, (init, init), unroll=2)

    ng = ng_ref[...]

    def finish(cidx, carry):
        rows = chunk_rows(cidx)
        h = h_ref[rows, :]
        mu = jnp.mean(h, axis=-1, keepdims=True)
        hc = h - mu
        var = jnp.mean(hc * hc, axis=-1, keepdims=True)
        y = hc * lax.rsqrt(var + NORM_EPS) * ng * _sigmoid(og_ref[rows, :])
        y_ref[rows, :] = y.astype(y_ref.dtype)
        return carry

    lax.fori_loop(0, nc, finish, 0, unroll=4)


def _mlstm(qk, z, gcol, grow, bias_r, bias_c, norm_g, *, batch, seq, heads, d, v_block0, o_block0,
           chunk=4 * SCAN_CHUNK):
    n = z.shape[0]
    assert n == batch * seq and seq % (2 * chunk) == 0
    block_bytes = 2 * seq * d * 2 + 2 * seq * d * 4 + 2 * seq * V7X_LANES * 4 + seq * d * 2
    scratch_bytes = seq * d * 4 + 2 * d * d * 4
    return pl.pallas_call(
        functools.partial(_mlstm_kernel, chunk=chunk),
        grid=(batch, heads),
        in_specs=[
            pl.BlockSpec((seq, d), lambda b, h: (b, h)),
            pl.BlockSpec((seq, d), lambda b, h: (b, heads + h)),
            pl.BlockSpec((seq, d), lambda b, h: (b, v_block0 + h)),
            pl.BlockSpec((seq, d), lambda b, h: (b, o_block0 + h)),
            pl.BlockSpec((None, None, seq, 4), lambda b, h: (b, h, 0, 0)),
            pl.BlockSpec((None, None, 4, seq), lambda b, h: (b, h, 0, 0)),
            pl.BlockSpec((None, 1, 4), lambda b, h: (h, 0, 0)),
            pl.BlockSpec((None, 4, 1), lambda b, h: (h, 0, 0)),
            pl.BlockSpec((1, d), lambda b, h: (0, h)),
        ],
        out_specs=pl.BlockSpec((seq, d), lambda b, h: (b, h)),
        out_shape=jax.ShapeDtypeStruct((n, heads * d), BF16),
        scratch_shapes=[pltpu.VMEM((seq, d), F32), pltpu.VMEM((2, d, d), F32)],
        compiler_params=pltpu.CompilerParams(
            dimension_semantics=("parallel", "parallel"),
            vmem_limit_bytes=_vmem_limit(block_bytes, scratch_bytes)),
        name="mlstm",
    )(qk, qk, z, z, gcol, grow, bias_r, bias_c, norm_g)


def _out_proj_kernel(x_ref, yh_ref, ym_ref, wh_ref, wm_ref, g_ref, b_ref, o_ref, whbf_ref, wmbf_ref, *, alpha):
    @pl.when(pl.program_id(0) == 0)
    def _():
        whbf_ref[...] = wh_ref[...].astype(BF16)
        wmbf_ref[...] = wm_ref[...].astype(BF16)

    r = (x_ref[...] * alpha
         + jnp.dot(yh_ref[...], whbf_ref[...], preferred_element_type=F32)
         + jnp.dot(ym_ref[...], wmbf_ref[...], preferred_element_type=F32))
    mu = jnp.mean(r, axis=-1, keepdims=True)
    d = r - mu
    var = jnp.mean(d * d, axis=-1, keepdims=True)
    o_ref[...] = d * lax.rsqrt(var + LN_EPS) * g_ref[...] + b_ref[...]


def _out_proj(x, yh, ym, w, g, b, *, alpha, tm=512):
    n, d = x.shape
    kh, km = yh.shape[1], ym.shape[1]
    assert n % tm == 0 and kh == km and w.shape[0] == kh + km
    block_bytes = tm * d * 4 + tm * kh * 2 + tm * km * 2 + tm * d * 4
    weight_bytes = (kh + km) * d * 4
    return pl.pallas_call(
        functools.partial(_out_proj_kernel, alpha=alpha),
        grid=(n // tm,),
        in_specs=[
            pl.BlockSpec((tm, d), lambda i: (i, 0)),
            pl.BlockSpec((tm, kh), lambda i: (i, 0)),
            pl.BlockSpec((tm, km), lambda i: (i, 0)),
            pl.BlockSpec((kh, d), lambda i: (0, 0), pipeline_mode=pl.Buffered(1)),
            pl.BlockSpec((km, d), lambda i: (1, 0), pipeline_mode=pl.Buffered(1)),
            pl.BlockSpec((1, d), lambda i: (0, 0)),
            pl.BlockSpec((1, d), lambda i: (0, 0)),
        ],
        out_specs=pl.BlockSpec((tm, d), lambda i: (i, 0)),
        out_shape=jax.ShapeDtypeStruct((n, d), F32),
        scratch_shapes=[pltpu.VMEM((kh, d), BF16), pltpu.VMEM((km, d), BF16)],
        compiler_params=pltpu.CompilerParams(
            dimension_semantics=("arbitrary",),
            vmem_limit_bytes=_vmem_limit(block_bytes, weight_bytes + weight_bytes // 2)),
        name="out_proj",
    )(x, yh, ym, w, w, g, b)


def kernel(x, ffn1_w1, ffn1_w3, ffn1_w2, ln1_g, ln1_b, w_in, hgrn_lb, hgrn_norm_g, mlstm_conv_w,
           mlstm_conv_b, mlstm_ig_b, mlstm_fg_b, mlstm_norm_g, w_out, ln2_g, ln2_b, ffn2_w1, ffn2_w3,
           ffn2_w2, ln3_g, ln3_b):
    batch, seq, d_model = x.shape
    depth = ffn1_w1.shape[0]
    assert depth == 1 and hgrn_lb.shape[1] == 2
    alpha = (2.0 * depth) ** 0.25
    hgrn_width = hgrn_norm_g.shape[1]
    mlstm_width = mlstm_norm_g.shape[1]
    hgrn_heads = hgrn_width // HGRN_HEAD_DIM
    md = mlstm_width // MLSTM_HEADS
    main_cols = 5 * hgrn_width + 4 * mlstm_width
    n = batch * seq

    def row(p):
        return p.reshape(1, -1).astype(F32)

    xf = x.reshape(n, d_model)
    x1 = _ffn_ln(xf, ffn1_w1[0], ffn1_w3[0], ffn1_w2[0], row(ln1_g[0]), row(ln1_b[0]), alpha=alpha)

    n_gates = w_in.shape[2] - main_cols
    zq, z, gates = _in_proj(x1, w_in[0].T, cols=main_cols, tn=hgrn_width)
    mlstm_col0 = 4 * hgrn_width

    y_h = _hgrn(zq, z, hgrn_lb.reshape(4, hgrn_width).astype(F32), row(hgrn_norm_g[0]),
                batch=batch, seq=seq, heads=hgrn_heads)

    qk = _qk_conv(z, mlstm_conv_w[0].astype(F32), row(mlstm_conv_b[0]), batch=batch, seq=seq,
                  col_block0=mlstm_col0 // 256, width=2 * mlstm_width, q_width=mlstm_width,
                  q_scale=md ** -0.5)

    g4 = gates[:, :n_gates].reshape(batch, seq, 4, MLSTM_HEADS)
    gcol = g4.transpose(0, 3, 1, 2)
    grow = g4.transpose(0, 3, 2, 1)
    bias4 = jnp.stack([mlstm_ig_b[0, 0], mlstm_ig_b[0, 1], mlstm_fg_b[0, 0], mlstm_fg_b[0, 1]],
                      axis=-1).astype(F32)
    y_m = _mlstm(qk, z, gcol, grow, bias4[:, None, :], bias4[:, :, None], row(mlstm_norm_g[0]),
                 batch=batch, seq=seq, heads=MLSTM_HEADS, d=md,
                 v_block0=(mlstm_col0 + 2 * mlstm_width) // md,
                 o_block0=(mlstm_col0 + 3 * mlstm_width) // md)

    x2 = _out_proj(x1, y_h, y_m, w_out[0], row(ln2_g[0]), row(ln2_b[0]), alpha=alpha)
    x3 = _ffn_ln(x2, ffn2_w1[0], ffn2_w3[0], ffn2_w2[0], row(ln3_g[0]), row(ln3_b[0]), alpha=alpha)
    return x3.reshape(batch, seq, d_model)
```
